```python
import jax, jax.numpy as jnp
from jax import lax
import numpy as np

D_MODEL = 2048
BATCH = 2
SEQ = 8192
DEPTH = 4

HEAD_DIM = 128
H_MLA = D_MODEL // (2 * HEAD_DIM)
Q_LORA_RANK = 3 * D_MODEL // 8
KV_LORA_RANK = D_MODEL // 4
QK_NOPE_DIM = 128
QK_ROPE_DIM = 64
QK_HEAD_DIM = QK_NOPE_DIM + QK_ROPE_DIM
V_HEAD_DIM = 128
H_DIL = D_MODEL // (2 * HEAD_DIM)
DIL_CONFIGS = ((128, 1), (512, 4), (2048, 16))
ROPE_THETA = 10000.0
Q_BLOCK = 128
D_FF_DENSE = 11 * D_MODEL // 4
N_EXPERTS = 8
TOP_K = 2
D_FF_EXPERT = 7 * D_MODEL // 2
N_DENSE = (DEPTH + 1) // 2
N_MOE = DEPTH // 2
EPS = 1e-6
NEG_INF = -1e30

MIX_A = H_MLA * V_HEAD_DIM
MIX_B = H_DIL * HEAD_DIM
MIX_WIDTH = MIX_A + MIX_B
D_IN = Q_LORA_RANK + KV_LORA_RANK + QK_ROPE_DIM + 3 * MIX_B
SPLIT_POINTS = (Q_LORA_RANK,
                Q_LORA_RANK + KV_LORA_RANK,
                Q_LORA_RANK + KV_LORA_RANK + QK_ROPE_DIM,
                Q_LORA_RANK + KV_LORA_RANK + QK_ROPE_DIM + MIX_B,
                Q_LORA_RANK + KV_LORA_RANK + QK_ROPE_DIM + 2 * MIX_B)

kernel_name = 'hymba_mla_dilated_moe_trunk'


def rmsnorm(x, g):
    xf = x.astype(jnp.float32)
    y = xf * lax.rsqrt(jnp.mean(xf * xf, axis=-1, keepdims=True) + EPS)
    return (y * g.astype(jnp.float32)).astype(x.dtype)


def rope_tables(seq, dim):
    inv = ROPE_THETA ** (-jnp.arange(0, dim, 2, dtype=jnp.float32) / dim)
    ang = jnp.arange(seq, dtype=jnp.float32)[:, None] * inv[None, :]
    return jnp.cos(ang), jnp.sin(ang)


def apply_rope(x, cos, sin):
    xf = x.astype(jnp.float32)
    x1, x2 = jnp.split(xf, 2, axis=-1)
    out = jnp.concatenate([x1 * cos - x2 * sin, x2 * cos + x1 * sin], axis=-1)
    return out.astype(x.dtype)


def blocks_to_seq(out):
    n, b, h, q, d = out.shape
    return out.transpose(1, 0, 3, 2, 4).reshape(b, n * q, h * d)


def mla_attention(q_nope, q_rope, k_nope, k_rope, v):
    S = q_nope.shape[2]
    scale = QK_HEAD_DIM ** -0.5
    kpos = jnp.arange(S)

    def block(i):
        start = i * Q_BLOCK
        qn = lax.dynamic_slice_in_dim(q_nope, start, Q_BLOCK, axis=2)
        qr = lax.dynamic_slice_in_dim(q_rope, start, Q_BLOCK, axis=2)
        s = (jnp.einsum('bhqd,bhkd->bhqk', qn, k_nope).astype(jnp.float32)
             + jnp.einsum('bhqr,bkr->bhqk', qr, k_rope).astype(jnp.float32)) * scale
        qpos = start + jnp.arange(Q_BLOCK)
        s = jnp.where(kpos[None, :] <= qpos[:, None], s, NEG_INF)
        p = jax.nn.softmax(s, axis=-1).astype(v.dtype)
        return jnp.einsum('bhqk,bhkd->bhqd', p, v)

    return blocks_to_seq(lax.map(block, jnp.arange(S // Q_BLOCK)))


def dilated_attention(q, k, v):
    S = q.shape[2]
    scale = HEAD_DIM ** -0.5

    def block(i):
        start = i * Q_BLOCK
        qb = lax.dynamic_slice_in_dim(q, start, Q_BLOCK, axis=2)
        qpos = start + jnp.arange(Q_BLOCK)
        ms, ls, outs = [], [], []
        for window, dil in DIL_CONFIGS:
            n_keys = window // dil + 1
            idx = qpos[:, None] - dil * jnp.arange(n_keys)[None, :]
            valid = idx >= 0
            idx = jnp.maximum(idx, 0)
            kg = jnp.take(k, idx, axis=2)
            vg = jnp.take(v, idx, axis=2)
            s = jnp.einsum('bhqd,bhqjd->bhqj', qb, kg).astype(jnp.float32) * scale
            s = jnp.where(valid, s, NEG_INF)
            m = jnp.max(s, axis=-1, keepdims=True)
            e = jnp.exp(s - m)
            l = jnp.sum(e, axis=-1, keepdims=True)
            o = jnp.einsum('bhqj,bhqjd->bhqd', (e / l).astype(v.dtype), vg)
            ms.append(m); ls.append(l); outs.append(o.astype(jnp.float32))
        m_all = jnp.stack(ms)
        w = jnp.stack(ls) * jnp.exp(m_all - jnp.max(m_all, axis=0))
        out = jnp.sum(w * jnp.stack(outs), axis=0) / jnp.sum(w, axis=0)
        return out.astype(v.dtype)

    return blocks_to_seq(lax.map(block, jnp.arange(S // Q_BLOCK)))


def mixer_layer(x, cos_r, sin_r, cos_f, sin_f, attn_norm_g, w_in, q_norm_g, kv_norm_g,
                w_uq, w_uk, w_uv, out_a_norm_g, out_b_norm_g, w_out):
    B, S, _ = x.shape
    h = rmsnorm(x, attn_norm_g)
    proj = h @ w_in
    c_q, c_kv, k_rope, q_b, k_b, v_b = jnp.split(proj, SPLIT_POINTS, axis=-1)

    q = (rmsnorm(c_q, q_norm_g) @ w_uq).reshape(B, S, H_MLA, QK_HEAD_DIM)
    q_nope = q[..., :QK_NOPE_DIM]
    q_rope = apply_rope(q[..., QK_NOPE_DIM:], cos_r[:, None, :], sin_r[:, None, :])
    c_kv = rmsnorm(c_kv, kv_norm_g)
    k_nope = (c_kv @ w_uk).reshape(B, S, H_MLA, QK_NOPE_DIM)
    v_a = (c_kv @ w_uv).reshape(B, S, H_MLA, V_HEAD_DIM)
    k_rope = apply_rope(k_rope, cos_r, sin_r)
    o_a = mla_attention(q_nope.transpose(0, 2, 1, 3), q_rope.transpose(0, 2, 1, 3),
                        k_nope.transpose(0, 2, 1, 3), k_rope, v_a.transpose(0, 2, 1, 3))

    q_b = apply_rope(q_b.reshape(B, S, H_DIL, HEAD_DIM), cos_f[:, None, :], sin_f[:, None, :])
    k_b = apply_rope(k_b.reshape(B, S, H_DIL, HEAD_DIM), cos_f[:, None, :], sin_f[:, None, :])
    v_b = v_b.reshape(B, S, H_DIL, HEAD_DIM)
    o_b = dilated_attention(q_b.transpose(0, 2, 1, 3), k_b.transpose(0, 2, 1, 3),
                            v_b.transpose(0, 2, 1, 3))

    o = jnp.concatenate([rmsnorm(o_a, out_a_norm_g), rmsnorm(o_b, out_b_norm_g)], axis=-1)
    return x + o @ w_out


def swiglu(h, w_gate, w_up, w_down):
    return (jax.nn.silu(h @ w_gate) * (h @ w_up)) @ w_down


def moe_ffn(h, router_w, w_gate, w_up, w_down):
    B, S, D = h.shape
    t = h.reshape(B * S, D)
    logits = (t @ router_w).astype(jnp.float32)
    top_v, top_i = lax.top_k(logits, TOP_K)
    top_w = jax.nn.softmax(top_v, axis=-1)
    gates = jnp.einsum('nk,nke->ne', top_w, jax.nn.one_hot(top_i, N_EXPERTS, dtype=jnp.float32))
    y = jnp.zeros_like(t)
    for e in range(N_EXPERTS):
        y = y + gates[:, e:e + 1].astype(t.dtype) * swiglu(t, w_gate[e], w_up[e], w_down[e])
    return y.reshape(B, S, D)


def setup_inputs(seed: int = 0) -> dict:
    key = jax.random.key(seed)
    ks = jax.random.split(key, 24)
    f32 = jnp.float32

    def w(k, shape, fan_in):
        return jax.random.normal(k, shape, f32) * (fan_in ** -0.5)

    def gain(k, shape):
        return 1.0 + 0.02 * jax.random.normal(k, shape, f32)

    L = DEPTH
    return {
        'x': jax.random.normal(ks[0], (BATCH, SEQ, D_MODEL), f32),
        'attn_norm_g': gain(ks[1], (L, D_MODEL)),
        'w_in': w(ks[2], (L, D_MODEL, D_IN), D_MODEL),
        'q_norm_g': gain(ks[3], (L, Q_LORA_RANK)),
        'kv_norm_g': gain(ks[4], (L, KV_LORA_RANK)),
        'w_uq': w(ks[5], (L, Q_LORA_RANK, H_MLA * QK_HEAD_DIM), Q_LORA_RANK),
        'w_uk': w(ks[6], (L, KV_LORA_RANK, H_MLA * QK_NOPE_DIM), KV_LORA_RANK),
        'w_uv': w(ks[7], (L, KV_LORA_RANK, H_MLA * V_HEAD_DIM), KV_LORA_RANK),
        'out_a_norm_g': gain(ks[8], (L, MIX_A)),
        'out_b_norm_g': gain(ks[9], (L, MIX_B)),
        'w_out': w(ks[10], (L, MIX_WIDTH, D_MODEL), MIX_WIDTH),
        'ffn_norm_g': gain(ks[11], (L, D_MODEL)),
        'dense_w_gate': w(ks[12], (N_DENSE, D_MODEL, D_FF_DENSE), D_MODEL),
        'dense_w_up': w(ks[13], (N_DENSE, D_MODEL, D_FF_DENSE), D_MODEL),
        'dense_w_down': w(ks[14], (N_DENSE, D_FF_DENSE, D_MODEL), D_FF_DENSE),
        'router_w': w(ks[15], (N_MOE, D_MODEL, N_EXPERTS), D_MODEL),
        'moe_w_gate': w(ks[16], (N_MOE, N_EXPERTS, D_MODEL, D_FF_EXPERT), D_MODEL),
        'moe_w_up': w(ks[17], (N_MOE, N_EXPERTS, D_MODEL, D_FF_EXPERT), D_MODEL),
        'moe_w_down': w(ks[18], (N_MOE, N_EXPERTS, D_FF_EXPERT, D_MODEL), D_FF_EXPERT),
        'final_norm_g': gain(ks[19], (D_MODEL,)),
    }


def reference(x, attn_norm_g, w_in, q_norm_g, kv_norm_g, w_uq, w_uk, w_uv, out_a_norm_g,
              out_b_norm_g, w_out, ffn_norm_g, dense_w_gate, dense_w_up, dense_w_down,
              router_w, moe_w_gate, moe_w_up, moe_w_down, final_norm_g):
    S = x.shape[1]
    cos_r, sin_r = rope_tables(S, QK_ROPE_DIM)
    cos_f, sin_f = rope_tables(S, HEAD_DIM)
    for layer in range(DEPTH):
        x = mixer_layer(x, cos_r, sin_r, cos_f, sin_f, attn_norm_g[layer], w_in[layer],
                        q_norm_g[layer], kv_norm_g[layer], w_uq[layer], w_uk[layer], w_uv[layer],
                        out_a_norm_g[layer], out_b_norm_g[layer], w_out[layer])
        h = rmsnorm(x, ffn_norm_g[layer])
        j = layer // 2
        if layer % 2 == 0:
            f = swiglu(h, dense_w_gate[j], dense_w_up[j], dense_w_down[j])
        else:
            f = moe_ffn(h, router_w[j], moe_w_gate[j], moe_w_up[j], moe_w_down[j])
        x = x + f
    return rmsnorm(x, final_norm_g)
```

```python
import functools

import jax
import jax.numpy as jnp
from jax import lax
from jax.experimental import pallas as pl
from jax.experimental.pallas import tpu as pltpu

F32 = jnp.float32
BF16 = jnp.bfloat16

LANES = 128
MLA_HEADS = 8
DIL_HEADS = 8
HEAD_DIM = 128
Q_LORA = 768
KV_LORA = 512
ROPE_DIM = 64
QK_DIM = HEAD_DIM + ROPE_DIM
N_EXPERTS = 8
DIL_CONFIGS = ((128, 1), (512, 4), (2048, 16))
DIL_KEYS = 128
ROPE_THETA = 10000.0
EPS = 1e-6
NEG_INF = -1e30
VMEM_LIMIT = 56 * 1024 * 1024


def _cparams(sem):
    return pltpu.CompilerParams(dimension_semantics=sem, vmem_limit_bytes=VMEM_LIMIT)


def _rms(xf, g):
    ms = jnp.mean(xf * xf, axis=-1, keepdims=True)
    return xf * lax.rsqrt(ms + EPS) * g


def _rope_full(a, c, s):
    return a * c + pltpu.roll(a, HEAD_DIM // 2, 1) * s


def _rope_half(a, c, s):
    lane = lax.broadcasted_iota(jnp.int32, a.shape, 1)
    lower = (lane % ROPE_DIM) < (ROPE_DIM // 2)
    partner = jnp.where(lower, pltpu.roll(a, LANES - ROPE_DIM // 2, 1),
                        pltpu.roll(a, ROPE_DIM // 2, 1))
    return a * c + partner * s


def _norm_mm_kernel(x_ref, g_ref, w_ref, cos_ref, sin_ref, o_ref, h_ref, *,
                    k_norm, rope_tiles, scale_tiles, scale):
    j = pl.program_id(1)

    @pl.when(j == 0)
    def _():
        xf = x_ref[:, :k_norm].astype(F32)
        h_ref[...] = _rms(xf, g_ref[...]).astype(BF16)

    acc = jnp.dot(h_ref[...], w_ref[...], preferred_element_type=F32)
    if rope_tiles == 0:
        o_ref[...] = acc.astype(o_ref.dtype)
        return

    @pl.when(j < rope_tiles)
    def _():
        scaled = j < scale_tiles
        c = jnp.where(scaled, cos_ref[...] * scale, cos_ref[...])
        s = jnp.where(scaled, sin_ref[...] * scale, sin_ref[...])
        for k in range(acc.shape[1] // HEAD_DIM):
            sl = slice(k * HEAD_DIM, (k + 1) * HEAD_DIM)
            o_ref[:, sl] = _rope_full(acc[:, sl], c, s).astype(o_ref.dtype)

    @pl.when(j >= rope_tiles)
    def _():
        o_ref[...] = acc.astype(o_ref.dtype)


def _norm_mm(x, g, w, cos, sin, *, x_block, k_norm, tm, tn, rope_tiles=0, scale_tiles=0,
             scale=1.0, name):
    n = x.shape[0]
    xw, xb = x_block
    kw, nout = w.shape
    assert kw == k_norm and n % tm == 0 and nout % tn == 0
    ns = cos.shape[0] // tm
    kern = functools.partial(_norm_mm_kernel, k_norm=k_norm, rope_tiles=rope_tiles,
                             scale_tiles=scale_tiles, scale=scale)
    return pl.pallas_call(
        kern,
        grid=(n // tm, nout // tn),
        in_specs=[
            pl.BlockSpec((tm, xw), lambda i, j: (i, xb)),
            pl.BlockSpec((1, k_norm), lambda i, j: (0, 0)),
            pl.BlockSpec((k_norm, tn), lambda i, j: (0, j)),
            pl.BlockSpec((tm, HEAD_DIM), lambda i, j: (i % ns, 0)),
            pl.BlockSpec((tm, HEAD_DIM), lambda i, j: (i % ns, 0)),
        ],
        out_specs=pl.BlockSpec((tm, tn), lambda i, j: (i, j)),
        out_shape=jax.ShapeDtypeStruct((n, nout), BF16),
        scratch_shapes=[pltpu.VMEM((tm, k_norm), BF16)],
        compiler_params=_cparams(("arbitrary", "arbitrary")),
        name=name,
    )(x, g.reshape(1, -1), w, cos, sin)


def _q_up_kernel(x_ref, g_ref, w_ref, cos_ref, sin_ref, q_ref, kr_ref, h_ref, *, scale):
    j = pl.program_id(1)
    c = cos_ref[...]
    s = sin_ref[...]

    @pl.when(j == 0)
    def _():
        blk = x_ref[...].astype(F32)
        h_ref[...] = _rms(blk[:, :Q_LORA], g_ref[...]).astype(BF16)
        kr_ref[...] = _rope_half(blk[:, Q_LORA:Q_LORA + LANES], c, s).astype(BF16)

    acc = jnp.dot(h_ref[...], w_ref[...], preferred_element_type=F32)
    for hh in range(acc.shape[1] // (2 * LANES)):
        lo = hh * 2 * LANES
        q_ref[:, lo:lo + LANES] = (acc[:, lo:lo + LANES] * scale).astype(BF16)
        q_ref[:, lo + LANES:lo + 2 * LANES] = (
            _rope_half(acc[:, lo + LANES:lo + 2 * LANES], c, s) * scale).astype(BF16)


def _q_up(lat, g, w, cos, sin, *, tm, tn, scale):
    n = lat.shape[0]
    nout = w.shape[1]
    ns = cos.shape[0] // tm
    return pl.pallas_call(
        functools.partial(_q_up_kernel, scale=scale),
        grid=(n // tm, nout // tn),
        in_specs=[
            pl.BlockSpec((tm, 1024), lambda i, j: (i, 0)),
            pl.BlockSpec((1, Q_LORA), lambda i, j: (0, 0)),
            pl.BlockSpec((Q_LORA, tn), lambda i, j: (0, j)),
            pl.BlockSpec((tm, LANES), lambda i, j: (i % ns, 0)),
            pl.BlockSpec((tm, LANES), lambda i, j: (i % ns, 0)),
        ],
        out_specs=[
            pl.BlockSpec((tm, tn), lambda i, j: (i, j)),
            pl.BlockSpec((tm, LANES), lambda i, j: (i, 0)),
        ],
        out_shape=[jax.ShapeDtypeStruct((n, nout), BF16),
                   jax.ShapeDtypeStruct((n, LANES), BF16)],
        scratch_shapes=[pltpu.VMEM((tm, Q_LORA), BF16)],
        compiler_params=_cparams(("arbitrary", "arbitrary")),
        name="q_up",
    )(lat, g.reshape(1, -1), w, cos, sin)


def _mla_kernel(q_ref, kn_ref, kr_ref, v_ref, o_ref, kcat_ref, m_ref, l_ref, acc_ref, *,
                tq, tk):
    qi = pl.program_id(2)

    @pl.when(qi == 0)
    def _():
        kcat_ref[:, :HEAD_DIM] = kn_ref[...]
        kcat_ref[:, HEAD_DIM:] = kr_ref[...]

    m_ref[...] = jnp.full(m_ref.shape, NEG_INF, F32)
    l_ref[...] = jnp.zeros(l_ref.shape, F32)
    acc_ref[...] = jnp.zeros(acc_ref.shape, F32)
    q = q_ref[...]
    per_q = tq // tk

    def chunk(j, masked):
        off = pl.multiple_of(j * tk, tk)
        k = kcat_ref[pl.ds(off, tk), :]
        v = v_ref[pl.ds(off, tk), :]
        s = lax.dot_general(q, k, (((1,), (1,)), ((), ())), preferred_element_type=F32)
        if masked:
            row = qi * tq + lax.broadcasted_iota(jnp.int32, (tq, tk), 0)
            col = off + lax.broadcasted_iota(jnp.int32, (tq, tk), 1)
            s = jnp.where(col <= row, s, NEG_INF)
        m_old = m_ref[...]
        m_new = jnp.maximum(m_old, jnp.max(s, axis=-1, keepdims=True))
        a = jnp.exp(m_old - m_new)
        p = jnp.exp(s - m_new)
        l_ref[...] = a * l_ref[...] + jnp.sum(p, axis=-1, keepdims=True)
        acc_ref[...] = a * acc_ref[...] + jnp.dot(p.astype(BF16), v,
                                                  preferred_element_type=F32)
        m_ref[...] = m_new

    def body(j, carry):
        chunk(j, False)
        return carry

    lax.fori_loop(0, qi * per_q, body, 0)
    for d in range(per_q):
        chunk(qi * per_q + d, True)
    o_ref[...] = (acc_ref[...] / l_ref[...]).astype(o_ref.dtype)


def _mla_attention(q, kv, kr, *, batch, seq, tq, tk):
    n = q.shape[0]
    nq = seq // tq
    return pl.pallas_call(
        functools.partial(_mla_kernel, tq=tq, tk=tk),
        grid=(batch, MLA_HEADS, nq),
        in_specs=[
            pl.BlockSpec((tq, 2 * LANES), lambda b, h, i: (b * nq + i, h)),
            pl.BlockSpec((seq, HEAD_DIM), lambda b, h, i: (b, h)),
            pl.BlockSpec((seq, LANES), lambda b, h, i: (b, 0)),
            pl.BlockSpec((seq, HEAD_DIM), lambda b, h, i: (b, MLA_HEADS + h)),
        ],
        out_specs=pl.BlockSpec((tq, HEAD_DIM), lambda b, h, i: (b * nq + i, h)),
        out_shape=jax.ShapeDtypeStruct((n, MLA_HEADS * HEAD_DIM), BF16),
        scratch_shapes=[
            pltpu.VMEM((seq, 2 * LANES), BF16),
            pltpu.VMEM((tq, 1), F32),
            pltpu.VMEM((tq, 1), F32),
            pltpu.VMEM((tq, HEAD_DIM), F32),
        ],
        compiler_params=_cparams(("arbitrary", "arbitrary", "arbitrary")),
        name="mla_attention",
    )(q, kv, kr, kv)


def _dil_kernel(q_ref, kp_ref, kc_ref, vp_ref, vc_ref, acc_ref, st_ref, *, tq):
    ti = pl.program_id(2)
    nk = tq + DIL_KEYS
    qpos = lax.broadcasted_iota(jnp.int32, (tq, nk), 0)
    kpos = lax.broadcasted_iota(jnp.int32, (tq, nk), 1)
    dist = qpos + DIL_KEYS - kpos
    valid = (dist >= 0) & (dist <= DIL_KEYS) & ((kpos >= DIL_KEYS) | (ti > 0))
    st_ref[...] = jnp.zeros(st_ref.shape, F32)
    for h in range(DIL_HEADS):
        sl = slice(h * HEAD_DIM, (h + 1) * HEAD_DIM)
        q = q_ref[0, :, sl]
        k = jnp.concatenate([kp_ref[0, :, sl], kc_ref[0, :, sl]], axis=0)
        v = jnp.concatenate([vp_ref[0, :, sl], vc_ref[0, :, sl]], axis=0)
        s = lax.dot_general(q, k, (((1,), (1,)), ((), ())), preferred_element_type=F32)
        s = jnp.where(valid, s, NEG_INF)
        m = jnp.max(s, axis=-1, keepdims=True)
        e = jnp.exp(s - m)
        l = jnp.sum(e, axis=-1, keepdims=True)
        acc_ref[0, :, sl] = jnp.dot(e.astype(BF16), v, preferred_element_type=F32)
        st_ref[0, :, h:h + 1] = m
        st_ref[0, :, DIL_HEADS + h:DIL_HEADS + h + 1] = l


def _dilated(qkv, *, batch, seq, dil, tq):
    t = seq // dil
    tq = min(tq, t)
    nt = t // tq
    cols = DIL_HEADS * HEAD_DIM
    view = qkv.reshape(batch, t, dil * 3 * cols)
    sub = tq // DIL_KEYS

    def prev(b, r, i):
        return (b, jnp.maximum(i * sub - 1, 0), 3 * r + 1)

    def prev_v(b, r, i):
        return (b, jnp.maximum(i * sub - 1, 0), 3 * r + 2)

    acc, st = pl.pallas_call(
        functools.partial(_dil_kernel, tq=tq),
        grid=(batch, dil, nt),
        in_specs=[
            pl.BlockSpec((1, tq, cols), lambda b, r, i: (b, i, 3 * r)),
            pl.BlockSpec((1, DIL_KEYS, cols), prev),
            pl.BlockSpec((1, tq, cols), lambda b, r, i: (b, i, 3 * r + 1)),
            pl.BlockSpec((1, DIL_KEYS, cols), prev_v),
            pl.BlockSpec((1, tq, cols), lambda b, r, i: (b, i, 3 * r + 2)),
        ],
        out_specs=[
            pl.BlockSpec((1, tq, cols), lambda b, r, i: (b, i, r)),
            pl.BlockSpec((1, tq, LANES), lambda b, r, i: (b, i, r)),
        ],
        out_shape=[jax.ShapeDtypeStruct((batch, t, dil * cols), F32),
                   jax.ShapeDtypeStruct((batch, t, dil * LANES), F32)],
        compiler_params=_cparams(("arbitrary", "arbitrary", "arbitrary")),
        name=f"dilated_d{dil}",
    )(view, view, view, view, view)
    return acc.reshape(batch * seq, cols), st.reshape(batch * seq, LANES)


def _out_proj_kernel(oa_ref, a1_ref, a2_ref, a3_ref, s1_ref, s2_ref, s3_ref, ga_ref, gb_ref,
                     w_ref, x_ref, o_ref, h_ref, ob_ref):
    j = pl.program_id(1)
    half = oa_ref.shape[1]

    @pl.when(j == 0)
    def _():
        for h in range(DIL_HEADS):
            sl = slice(h * HEAD_DIM, (h + 1) * HEAD_DIM)
            ms = [s[:, h:h + 1] for s in (s1_ref, s2_ref, s3_ref)]
            ls = [s[:, DIL_HEADS + h:DIL_HEADS + h + 1] for s in (s1_ref, s2_ref, s3_ref)]
            m_all = jnp.maximum(jnp.maximum(ms[0], ms[1]), ms[2])
            ws = [jnp.exp(m - m_all) for m in ms]
            den = ws[0] * ls[0] + ws[1] * ls[1] + ws[2] * ls[2]
            num = ws[0] * a1_ref[:, sl] + ws[1] * a2_ref[:, sl] + ws[2] * a3_ref[:, sl]
            ob_ref[:, sl] = num / den
        h_ref[:, :half] = _rms(oa_ref[...].astype(F32), ga_ref[...]).astype(BF16)
        h_ref[:, half:] = _rms(ob_ref[...], gb_ref[...]).astype(BF16)

    o_ref[...] = x_ref[...] + jnp.dot(h_ref[...], w_ref[...], preferred_element_type=F32)


def _out_proj(o_a, accs, stats, g_a, g_b, w, x, *, tm, tn):
    n, d = x.shape
    half = o_a.shape[1]
    row = lambda i, j: (i, 0)
    return pl.pallas_call(
        _out_proj_kernel,
        grid=(n // tm, d // tn),
        in_specs=[pl.BlockSpec((tm, half), row)]
        + [pl.BlockSpec((tm, half), row)] * 3
        + [pl.BlockSpec((tm, LANES), row)] * 3
        + [pl.BlockSpec((1, half), lambda i, j: (0, 0))] * 2
        + [pl.BlockSpec((2 * half, tn), lambda i, j: (0, j)),
           pl.BlockSpec((tm, tn), lambda i, j: (i, j))],
        out_specs=pl.BlockSpec((tm, tn), lambda i, j: (i, j)),
        out_shape=jax.ShapeDtypeStruct((n, d), F32),
        scratch_shapes=[pltpu.VMEM((tm, 2 * half), BF16), pltpu.VMEM((tm, half), F32)],
        compiler_params=_cparams(("arbitrary", "arbitrary")),
        name="out_proj",
    )(o_a, *accs, *stats, g_a.reshape(1, -1), g_b.reshape(1, -1), w, x)


def _gate_up_kernel(x_ref, g_ref, wg_ref, wu_ref, o_ref, h_ref):
    @pl.when(pl.program_id(1) == 0)
    def _():
        h_ref[...] = _rms(x_ref[...], g_ref[...]).astype(BF16)

    h = h_ref[...]
    gate = jnp.dot(h, wg_ref[...], preferred_element_type=F32)
    up = jnp.dot(h, wu_ref[...], preferred_element_type=F32)
    o_ref[...] = (gate * jax.nn.sigmoid(gate) * up).astype(o_ref.dtype)


def _gate_up(x, g, wg, wu, *, tm, tn):
    n, d = x.shape
    f = wg.shape[1]
    return pl.pallas_call(
        _gate_up_kernel,
        grid=(n // tm, f // tn),
        in_specs=[
            pl.BlockSpec((tm, d), lambda i, j: (i, 0)),
            pl.BlockSpec((1, d), lambda i, j: (0, 0)),
            pl.BlockSpec((d, tn), lambda i, j: (0, j)),
            pl.BlockSpec((d, tn), lambda i, j: (0, j)),
        ],
        out_specs=pl.BlockSpec((tm, tn), lambda i, j: (i, j)),
        out_shape=jax.ShapeDtypeStruct((n, f), BF16),
        scratch_shapes=[pltpu.VMEM((tm, d), BF16)],
        compiler_params=_cparams(("arbitrary", "arbitrary")),
        name="ffn_gate_up",
    )(x, g.reshape(1, -1), wg, wu)


def _down_kernel(a_ref, w_ref, x_ref, o_ref):
    o_ref[...] = x_ref[...] + jnp.dot(a_ref[...], w_ref[...], preferred_element_type=F32)


def _down(act, w, x, *, tm, tn):
    n, f = act.shape
    d = w.shape[1]
    return pl.pallas_call(
        _down_kernel,
        grid=(n // tm, d // tn),
        in_specs=[
            pl.BlockSpec((tm, f), lambda i, j: (i, 0)),
            pl.BlockSpec((f, tn), lambda i, j: (0, j)),
            pl.BlockSpec((tm, tn), lambda i, j: (i, j)),
        ],
        out_specs=pl.BlockSpec((tm, tn), lambda i, j: (i, j)),
        out_shape=jax.ShapeDtypeStruct((n, d), F32),
        compiler_params=_cparams(("arbitrary", "arbitrary")),
        name="ffn_down",
    )(act, w, x)


def _router_kernel(x_ref, g_ref, w_ref, idx_ref, wt_ref):
    h = _rms(x_ref[...], g_ref[...])
    logits = jnp.dot(h, w_ref[...], preferred_element_type=F32,
                     precision=lax.Precision.HIGHEST)
    lane_i = lax.broadcasted_iota(jnp.int32, logits.shape, 1)
    lane = lane_i.astype(F32)
    logits = jnp.where(lane_i < N_EXPERTS, logits, -jnp.inf)
    v1 = jnp.max(logits, axis=-1, keepdims=True)
    i1 = jnp.min(jnp.where(logits == v1, lane, float(LANES)), axis=-1, keepdims=True)
    rest = jnp.where(lane == i1, -jnp.inf, logits)
    v2 = jnp.max(rest, axis=-1, keepdims=True)
    i2 = jnp.min(jnp.where(rest == v2, lane, float(LANES)), axis=-1, keepdims=True)
    e2 = jnp.exp(v2 - v1)
    w1 = 1.0 / (1.0 + e2)
    w2 = e2 / (1.0 + e2)
    idx_ref[...] = jnp.where(lane_i == 0, i1, jnp.where(lane_i == 1, i2, 0.0)).astype(jnp.int32)
    wt_ref[...] = jnp.where(lane_i == 0, w1, jnp.where(lane_i == 1, w2, 0.0))


def _router(x, g, w_pad, *, tm):
    n, d = x.shape
    return pl.pallas_call(
        _router_kernel,
        grid=(n // tm,),
        in_specs=[
            pl.BlockSpec((tm, d), lambda i: (i, 0)),
            pl.BlockSpec((1, d), lambda i: (0, 0)),
            pl.BlockSpec((d, LANES), lambda i: (0, 0)),
        ],
        out_specs=[pl.BlockSpec((tm, LANES), lambda i: (i, 0))] * 2,
        out_shape=[jax.ShapeDtypeStruct((n, LANES), jnp.int32),
                   jax.ShapeDtypeStruct((n, LANES), F32)],
        compiler_params=_cparams(("arbitrary",)),
        name="moe_router",
    )(x, g.reshape(1, -1), w_pad)


def _row_copy(src_hbm, row, dst_vmem, slot, sem):
    return pltpu.make_async_copy(src_hbm.at[pl.ds(row, 1)], dst_vmem.at[pl.ds(slot, 1)], sem)


def _gather_rows(idx_smem, src_hbm, dst_vmem, sem, rows):
    def start(i, c):
        _row_copy(src_hbm, idx_smem[i], dst_vmem, i, sem).start()
        return c

    def wait(i, c):
        _row_copy(src_hbm, 0, dst_vmem, i, sem).wait()
        return c

    lax.fori_loop(0, rows, start, 0)
    lax.fori_loop(0, rows, wait, 0)


def _gather_norm_kernel(idx_hbm, x_hbm, g_ref, o_ref, idx_smem, buf, isem, sem):
    i = pl.program_id(0)
    cp = pltpu.make_async_copy(idx_hbm.at[i], idx_smem, isem)
    cp.start()
    cp.wait()
    _gather_rows(idx_smem, x_hbm, buf, sem, buf.shape[0])
    o_ref[...] = _rms(buf[...], g_ref[...]).astype(o_ref.dtype)


def _gather_norm(row_token, x, g, *, tr):
    nt = row_token.shape[0]
    d = x.shape[1]
    return pl.pallas_call(
        _gather_norm_kernel,
        grid=(nt,),
        in_specs=[
            pl.BlockSpec(memory_space=pl.ANY),
            pl.BlockSpec(memory_space=pl.ANY),
            pl.BlockSpec((1, d), lambda i: (0, 0)),
        ],
        out_specs=pl.BlockSpec((tr, d), lambda i: (i, 0)),
        out_shape=jax.ShapeDtypeStruct((nt * tr, d), BF16),
        scratch_shapes=[
            pltpu.SMEM((tr,), jnp.int32),
            pltpu.VMEM((tr, d), F32),
            pltpu.SemaphoreType.DMA,
            pltpu.SemaphoreType.DMA,
        ],
        compiler_params=_cparams(("arbitrary",)),
        name="moe_gather",
    )(row_token, x, g.reshape(1, -1))


def _moe_gate_up_kernel(te_ref, nu_ref, x_ref, wg_ref, wu_ref, o_ref):
    i = pl.program_id(1)

    @pl.when(i < nu_ref[0])
    def _():
        h = x_ref[...]
        gate = jnp.dot(h, wg_ref[0], preferred_element_type=F32)
        up = jnp.dot(h, wu_ref[0], preferred_element_type=F32)
        o_ref[...] = (gate * jax.nn.sigmoid(gate) * up).astype(o_ref.dtype)

    @pl.when(i >= nu_ref[0])
    def _():
        o_ref[...] = jnp.zeros(o_ref.shape, o_ref.dtype)


def _moe_gate_up(tile_expert, n_used, xs, wg, wu, *, tm, tn):
    r, d = xs.shape
    f = wg.shape[2]
    grid_spec = pltpu.PrefetchScalarGridSpec(
        num_scalar_prefetch=2,
        grid=(f // tn, r // tm),
        in_specs=[
            pl.BlockSpec((tm, d), lambda j, i, te, nu: (i, 0)),
            pl.BlockSpec((1, d, tn), lambda j, i, te, nu: (te[i], 0, j)),
            pl.BlockSpec((1, d, tn), lambda j, i, te, nu: (te[i], 0, j)),
        ],
        out_specs=pl.BlockSpec((tm, tn), lambda j, i, te, nu: (i, j)),
    )
    return pl.pallas_call(
        _moe_gate_up_kernel,
        grid_spec=grid_spec,
        out_shape=jax.ShapeDtypeStruct((r, f), BF16),
        compiler_params=_cparams(("arbitrary", "arbitrary")),
        name="moe_gate_up",
    )(tile_expert, n_used, xs, wg, wu)


def _moe_down_kernel(te_ref, nu_ref, a_ref, w_ref, rw_ref, o_ref):
    i = pl.program_id(1)

    @pl.when(i < nu_ref[0])
    def _():
        y = jnp.dot(a_ref[...], w_ref[0], preferred_element_type=F32)
        o_ref[...] = y * rw_ref[...]

    @pl.when(i >= nu_ref[0])
    def _():
        o_ref[...] = jnp.zeros(o_ref.shape, o_ref.dtype)


def _moe_down(tile_expert, n_used, act, w, row_w, *, tm, tn):
    r, f = act.shape
    d = w.shape[2]
    grid_spec = pltpu.PrefetchScalarGridSpec(
        num_scalar_prefetch=2,
        grid=(d // tn, r // tm),
        in_specs=[
            pl.BlockSpec((tm, f), lambda j, i, te, nu: (i, 0)),
            pl.BlockSpec((1, f, tn), lambda j, i, te, nu: (te[i], 0, j)),
            pl.BlockSpec((tm, 1), lambda j, i, te, nu: (i, 0)),
        ],
        out_specs=pl.BlockSpec((tm, tn), lambda j, i, te, nu: (i, j)),
    )
    return pl.pallas_call(
        _moe_down_kernel,
        grid_spec=grid_spec,
        out_shape=jax.ShapeDtypeStruct((r, d), F32),
        compiler_params=_cparams(("arbitrary", "arbitrary")),
        name="moe_down",
    )(tile_expert, n_used, act, w, row_w)


def _combine_kernel(p0_hbm, p1_hbm, ys_hbm, x_ref, g_ref, o_ref, i0_smem, i1_smem, b0, b1,
                    isem, sem0, sem1, *, final_norm):
    i = pl.program_id(0)
    c0 = pltpu.make_async_copy(p0_hbm.at[i], i0_smem, isem)
    c0.start()
    c0.wait()
    c1 = pltpu.make_async_copy(p1_hbm.at[i], i1_smem, isem)
    c1.start()
    c1.wait()
    rows = b0.shape[0]

    def start(t, c):
        _row_copy(ys_hbm, i0_smem[t], b0, t, sem0).start()
        _row_copy(ys_hbm, i1_smem[t], b1, t, sem1).start()
        return c

    def wait(t, c):
        _row_copy(ys_hbm, 0, b0, t, sem0).wait()
        _row_copy(ys_hbm, 0, b1, t, sem1).wait()
        return c

    lax.fori_loop(0, rows, start, 0)
    lax.fori_loop(0, rows, wait, 0)
    y = x_ref[...] + (b0[...] + b1[...])
    if final_norm:
        y = _rms(y, g_ref[...])
    o_ref[...] = y


def _combine(pos0, pos1, ys, x, g, *, tr, final_norm):
    n, d = x.shape
    return pl.pallas_call(
        functools.partial(_combine_kernel, final_norm=final_norm),
        grid=(n // tr,),
        in_specs=[
            pl.BlockSpec(memory_space=pl.ANY),
            pl.BlockSpec(memory_space=pl.ANY),
            pl.BlockSpec(memory_space=pl.ANY),
            pl.BlockSpec((tr, d), lambda i: (i, 0)),
            pl.BlockSpec((1, d), lambda i: (0, 0)),
        ],
        out_specs=pl.BlockSpec((tr, d), lambda i: (i, 0)),
        out_shape=jax.ShapeDtypeStruct((n, d), F32),
        scratch_shapes=[
            pltpu.SMEM((tr,), jnp.int32),
            pltpu.SMEM((tr,), jnp.int32),
            pltpu.VMEM((tr, d), F32),
            pltpu.VMEM((tr, d), F32),
            pltpu.SemaphoreType.DMA,
            pltpu.SemaphoreType.DMA,
            pltpu.SemaphoreType.DMA,
        ],
        compiler_params=_cparams(("arbitrary",)),
        name="moe_combine",
    )(pos0.reshape(n // tr, tr), pos1.reshape(n // tr, tr), ys, x, g.reshape(1, -1))


def _rmsnorm_kernel(x_ref, g_ref, o_ref):
    o_ref[...] = _rms(x_ref[...], g_ref[...])


def _final_norm(x, g, *, tm):
    n, d = x.shape
    return pl.pallas_call(
        _rmsnorm_kernel,
        grid=(n // tm,),
        in_specs=[pl.BlockSpec((tm, d), lambda i: (i, 0)),
                  pl.BlockSpec((1, d), lambda i: (0, 0))],
        out_specs=pl.BlockSpec((tm, d), lambda i: (i, 0)),
        out_shape=jax.ShapeDtypeStruct((n, d), F32),
        compiler_params=_cparams(("arbitrary",)),
        name="final_norm",
    )(x, g.reshape(1, -1))


def _routing_plan(top_idx, top_w, *, tm):
    n = top_idx.shape[0]
    n_assign = 2 * n
    rows = n_assign + N_EXPERTS * tm
    n_tiles = rows // tm
    e_flat = top_idx.reshape(n_assign)
    onehot = (e_flat[:, None] == jnp.arange(N_EXPERTS, dtype=jnp.int32)[None, :]).astype(jnp.int32)
    counts = jnp.sum(onehot, axis=0)
    rank = jnp.sum((jnp.cumsum(onehot, axis=0) - onehot) * onehot, axis=1)
    padded = ((counts + tm - 1) // tm) * tm
    ends = jnp.cumsum(padded)
    starts = ends - padded
    dest = starts[e_flat] + rank
    token = jnp.arange(n_assign, dtype=jnp.int32) // 2
    row_token = jnp.zeros((rows,), jnp.int32).at[dest].set(token)
    row_w = jnp.zeros((rows,), F32).at[dest].set(top_w.reshape(n_assign))
    tile_start = jnp.arange(n_tiles, dtype=jnp.int32) * tm
    tile_expert = jnp.sum((tile_start[:, None] >= ends[None, :]).astype(jnp.int32), axis=1)
    tile_expert = jnp.minimum(tile_expert, N_EXPERTS - 1).astype(jnp.int32)
    n_used = (ends[-1] // tm).astype(jnp.int32).reshape(1)
    pos = dest.reshape(n, 2).astype(jnp.int32)
    return row_token.reshape(n_tiles, tm), row_w.reshape(rows, 1), tile_expert, n_used, pos


def _moe_layer(x, g, router_pad, wg, wu, wd, final_g, *, final_norm):
    tm = 512
    idx, wt = _router(x, g, router_pad, tm=1024)
    row_token, row_w, tile_expert, n_used, pos = _routing_plan(idx[:, :2], wt[:, :2], tm=tm)
    xs = _gather_norm(row_token, x, g, tr=tm)
    act = _moe_gate_up(tile_expert, n_used, xs, wg, wu, tm=tm, tn=1024)
    ys = _moe_down(tile_expert, n_used, act, wd, row_w, tm=tm, tn=512)
    return _combine(pos[:, 0], pos[:, 1], ys, x, final_g, tr=256, final_norm=final_norm)


def _rope_tables(seq):
    pos = jnp.arange(seq, dtype=F32)[:, None]

    def tab(dim):
        inv = ROPE_THETA ** (-jnp.arange(0, dim, 2, dtype=F32) / dim)
        ang = pos * inv[None, :]
        return jnp.cos(ang), jnp.sin(ang)

    cf, sf = tab(HEAD_DIM)
    cos_f = jnp.concatenate([cf, cf], axis=1)
    sin_f = jnp.concatenate([-sf, sf], axis=1)
    cr, sr = tab(ROPE_DIM)
    zero = jnp.zeros((seq, LANES - ROPE_DIM), F32)
    cos_r = jnp.concatenate([cr, cr, zero], axis=1)
    sin_r = jnp.concatenate([-sr, sr, zero], axis=1)
    return cos_f, sin_f, cos_r, sin_r


def kernel(x, attn_norm_g, w_in, q_norm_g, kv_norm_g, w_uq, w_uk, w_uv, out_a_norm_g,
           out_b_norm_g, w_out, ffn_norm_g, dense_w_gate, dense_w_up, dense_w_down,
           router_w, moe_w_gate, moe_w_up, moe_w_down, final_norm_g):
    batch, seq, d_model = x.shape
    depth = w_in.shape[0]
    n = batch * seq
    mix = DIL_HEADS * HEAD_DIM
    cos_f, sin_f, cos_r, sin_r = _rope_tables(seq)

    c0, c1, c2 = Q_LORA, Q_LORA + KV_LORA, Q_LORA + KV_LORA + ROPE_DIM
    pad = jnp.zeros((depth, d_model, 1024 - Q_LORA - ROPE_DIM), F32)
    w_lat = jnp.concatenate([w_in[:, :, :c0], w_in[:, :, c1:c2], pad, w_in[:, :, c0:c1]],
                            axis=2).astype(BF16)
    w_qkv = w_in[:, :, c2:].astype(BF16)
    uq = w_uq.reshape(depth, Q_LORA, MLA_HEADS, QK_DIM)
    uq = jnp.concatenate([uq, jnp.zeros((depth, Q_LORA, MLA_HEADS, 2 * LANES - QK_DIM), F32)],
                         axis=3)
    w_uq_p = uq.reshape(depth, Q_LORA, MLA_HEADS * 2 * LANES).astype(BF16)
    w_ukv = jnp.concatenate([w_uk, w_uv], axis=2).astype(BF16)
    w_out_b = w_out.astype(BF16)
    dense_g = dense_w_gate.astype(BF16)
    dense_u = dense_w_up.astype(BF16)
    dense_d = dense_w_down.astype(BF16)
    moe_g = moe_w_gate.astype(BF16)
    moe_u = moe_w_up.astype(BF16)
    moe_d = moe_w_down.astype(BF16)
    router_pad = jnp.concatenate(
        [router_w, jnp.zeros(router_w.shape[:2] + (LANES - N_EXPERTS,), F32)], axis=2)

    xf = x.reshape(n, d_model)
    for layer in range(depth):
        g_attn = attn_norm_g[layer]
        lat = _norm_mm(xf, g_attn, w_lat[layer], cos_f, sin_f, x_block=(d_model, 0),
                       k_norm=d_model, tm=1024, tn=512, name="in_proj_latent")
        qkv = _norm_mm(xf, g_attn, w_qkv[layer], cos_f, sin_f, x_block=(d_model, 0),
                       k_norm=d_model, tm=1024, tn=512, rope_tiles=2 * mix // 512,
                       scale_tiles=mix // 512, scale=HEAD_DIM ** -0.5, name="in_proj_qkv")
        q, kr = _q_up(lat, q_norm_g[layer], w_uq_p[layer], cos_r, sin_r, tm=1024, tn=512,
                      scale=QK_DIM ** -0.5)
        kv = _norm_mm(lat, kv_norm_g[layer], w_ukv[layer], cos_f, sin_f,
                      x_block=(KV_LORA, 1024 // KV_LORA), k_norm=KV_LORA, tm=1024, tn=1024,
                      name="kv_up")
        o_a = _mla_attention(q, kv, kr, batch=batch, seq=seq, tq=1024, tk=512)
        accs, stats = [], []
        for _, dil in DIL_CONFIGS:
            a, s = _dilated(qkv, batch=batch, seq=seq, dil=dil, tq=256)
            accs.append(a)
            stats.append(s)
        xf = _out_proj(o_a, accs, stats, out_a_norm_g[layer], out_b_norm_g[layer],
                       w_out_b[layer], xf, tm=512, tn=512)
        j = layer // 2
        last = layer == depth - 1
        if layer % 2 == 0:
            act = _gate_up(xf, ffn_norm_g[layer], dense_g[j], dense_u[j], tm=1024, tn=512)
            xf = _down(act, dense_d[j], xf, tm=1024, tn=512)
            if last:
                xf = _final_norm(xf, final_norm_g, tm=1024)
        else:
            xf = _moe_layer(xf, ffn_norm_g[layer], router_pad[j], moe_g[j], moe_u[j], moe_d[j],
                            final_norm_g, final_norm=last)
    return xf.reshape(batch, seq, d_model)
```

```python
import functools

import jax
import jax.numpy as jnp
from jax import lax
from jax.experimental import pallas as pl
from jax.experimental.pallas import tpu as pltpu

F32 = jnp.float32
BF16 = jnp.bfloat16

LANES = 128
BF16_SUBLANES = 16
MLA_HEADS = 8
DIL_HEADS = 8
HEAD_DIM = 128
Q_LORA = 768
KV_LORA = 512
ROPE_DIM = 64
QK_DIM = HEAD_DIM + ROPE_DIM
N_EXPERTS = 8
DIL_CONFIGS = ((128, 1), (512, 4), (2048, 16))
DIL_KEYS = 128
ROPE_THETA = 10000.0
EPS = 1e-6
NEG_INF = -1e30
LOG2_E = 1.4426950408889634
VMEM_LIMIT = 56 * 1024 * 1024


def _cparams(sem):
    return pltpu.CompilerParams(dimension_semantics=sem, vmem_limit_bytes=VMEM_LIMIT)


def _rms(xf, g):
    ms = jnp.mean(xf * xf, axis=-1, keepdims=True)
    return xf * lax.rsqrt(ms + EPS) * g


def _rope_full(a, c, s):
    return a * c + pltpu.roll(a, HEAD_DIM // 2, 1) * s


def _rope_half(a, c, s):
    lane = lax.broadcasted_iota(jnp.int32, a.shape, 1)
    lower = (lane % ROPE_DIM) < (ROPE_DIM // 2)
    partner = jnp.where(lower, pltpu.roll(a, LANES - ROPE_DIM // 2, 1),
                        pltpu.roll(a, ROPE_DIM // 2, 1))
    return a * c + partner * s


def _norm_mm_kernel(x_ref, g_ref, w_ref, cos_ref, sin_ref, o_ref, h_ref, *,
                    k_norm, rope_tiles, scale_tiles, scale):
    j = pl.program_id(1)

    @pl.when(j == 0)
    def _():
        xf = x_ref[:, :k_norm].astype(F32)
        h_ref[...] = _rms(xf, g_ref[...]).astype(BF16)

    acc = jnp.dot(h_ref[...], w_ref[...], preferred_element_type=F32)
    if rope_tiles == 0:
        o_ref[...] = acc.astype(o_ref.dtype)
        return

    @pl.when(j < rope_tiles)
    def _():
        scaled = j < scale_tiles
        c = jnp.where(scaled, cos_ref[...] * scale, cos_ref[...])
        s = jnp.where(scaled, sin_ref[...] * scale, sin_ref[...])
        for k in range(acc.shape[1] // HEAD_DIM):
            sl = slice(k * HEAD_DIM, (k + 1) * HEAD_DIM)
            o_ref[:, sl] = _rope_full(acc[:, sl], c, s).astype(o_ref.dtype)

    @pl.when(j >= rope_tiles)
    def _():
        o_ref[...] = acc.astype(o_ref.dtype)


def _norm_mm(x, g, w, cos, sin, *, x_block, k_norm, tm, tn, rope_tiles=0, scale_tiles=0,
             scale=1.0, name):
    n = x.shape[0]
    xw, xb = x_block
    kw, nout = w.shape
    assert kw == k_norm and n % tm == 0 and nout % tn == 0
    ns = cos.shape[0] // tm
    kern = functools.partial(_norm_mm_kernel, k_norm=k_norm, rope_tiles=rope_tiles,
                             scale_tiles=scale_tiles, scale=scale)
    return pl.pallas_call(
        kern,
        grid=(n // tm, nout // tn),
        in_specs=[
            pl.BlockSpec((tm, xw), lambda i, j: (i, xb)),
            pl.BlockSpec((1, k_norm), lambda i, j: (0, 0)),
            pl.BlockSpec((k_norm, tn), lambda i, j: (0, j)),
            pl.BlockSpec((tm, HEAD_DIM), lambda i, j: (i % ns, 0)),
            pl.BlockSpec((tm, HEAD_DIM), lambda i, j: (i % ns, 0)),
        ],
        out_specs=pl.BlockSpec((tm, tn), lambda i, j: (i, j)),
        out_shape=jax.ShapeDtypeStruct((n, nout), BF16),
        scratch_shapes=[pltpu.VMEM((tm, k_norm), BF16)],
        compiler_params=_cparams(("arbitrary", "arbitrary")),
        name=name,
    )(x, g.reshape(1, -1), w, cos, sin)


def _q_up_kernel(x_ref, g_ref, w_ref, cos_ref, sin_ref, q_ref, kr_ref, h_ref, *, scale):
    j = pl.program_id(1)
    c = cos_ref[...]
    s = sin_ref[...]

    @pl.when(j == 0)
    def _():
        blk = x_ref[...].astype(F32)
        h_ref[...] = _rms(blk[:, :Q_LORA], g_ref[...]).astype(BF16)
        kr_ref[...] = _rope_half(blk[:, Q_LORA:Q_LORA + LANES], c, s).astype(BF16)

    acc = jnp.dot(h_ref[...], w_ref[...], preferred_element_type=F32)
    for hh in range(acc.shape[1] // (2 * LANES)):
        lo = hh * 2 * LANES
        q_ref[:, lo:lo + LANES] = (acc[:, lo:lo + LANES] * scale).astype(BF16)
        q_ref[:, lo + LANES:lo + 2 * LANES] = (
            _rope_half(acc[:, lo + LANES:lo + 2 * LANES], c, s) * scale).astype(BF16)


def _q_up(lat, g, w, cos, sin, *, tm, tn, scale):
    n = lat.shape[0]
    nout = w.shape[1]
    ns = cos.shape[0] // tm
    return pl.pallas_call(
        functools.partial(_q_up_kernel, scale=scale),
        grid=(n // tm, nout // tn),
        in_specs=[
            pl.BlockSpec((tm, 1024), lambda i, j: (i, 0)),
            pl.BlockSpec((1, Q_LORA), lambda i, j: (0, 0)),
            pl.BlockSpec((Q_LORA, tn), lambda i, j: (0, j)),
            pl.BlockSpec((tm, LANES), lambda i, j: (i % ns, 0)),
            pl.BlockSpec((tm, LANES), lambda i, j: (i % ns, 0)),
        ],
        out_specs=[
            pl.BlockSpec((tm, tn), lambda i, j: (i, j)),
            pl.BlockSpec((tm, LANES), lambda i, j: (i, 0)),
        ],
        out_shape=[jax.ShapeDtypeStruct((n, nout), BF16),
                   jax.ShapeDtypeStruct((n, LANES), BF16)],
        scratch_shapes=[pltpu.VMEM((tm, Q_LORA), BF16)],
        compiler_params=_cparams(("arbitrary", "arbitrary")),
        name="q_up",
    )(lat, g.reshape(1, -1), w, cos, sin)


def _mla_kernel(q_ref, kn_ref, kr_ref, v_ref, o_ref, kcat_ref, vt_ref, qt_ref, m_ref,
                acc_ref, s_ref, *, tq, tk):
    qi = pl.program_id(2)
    seq = kcat_ref.shape[0]
    vt_rows = vt_ref.shape[1]

    @pl.when(qi == 0)
    def _():
        kcat_ref[:, :HEAD_DIM] = kn_ref[...]
        kcat_ref[:, HEAD_DIM:] = kr_ref[...]
        ones_row = (lax.broadcasted_iota(jnp.int32, (vt_rows - HEAD_DIM, tk), 0) == 0)

        def transpose_chunk(c, carry):
            off = pl.multiple_of(c * tk, tk)
            vt_ref[c, :HEAD_DIM, :] = v_ref[pl.ds(off, tk), :].astype(F32).T.astype(BF16)
            vt_ref[c, HEAD_DIM:, :] = ones_row.astype(BF16)
            return carry

        lax.fori_loop(0, seq // tk, transpose_chunk, 0)

    qt_ref[...] = q_ref[...].astype(F32).T.astype(BF16)
    m_ref[...] = jnp.full(m_ref.shape, NEG_INF, F32)
    acc_ref[...] = jnp.zeros(acc_ref.shape, F32)
    per_q = tq // tk

    def scores(j):
        off = pl.multiple_of(j * tk, tk)
        return jnp.dot(kcat_ref[pl.ds(off, tk), :], qt_ref[...],
                       preferred_element_type=F32)

    def chunk(j, masked, has_next):
        s = s_ref[j % 2]
        if has_next:
            s_ref[(j + 1) % 2] = scores(j + 1)
        if masked:
            key = j * tk + lax.broadcasted_iota(jnp.int32, (tk, tq), 0)
            qry = qi * tq + lax.broadcasted_iota(jnp.int32, (tk, tq), 1)
            s = jnp.where(key <= qry, s, NEG_INF)
        m_old = m_ref[...]
        m_new = jnp.maximum(m_old, jnp.max(s, axis=0, keepdims=True))
        a = jnp.exp2(m_old - m_new)
        p = jnp.exp2(s - m_new).astype(BF16)
        acc_ref[...] = a * acc_ref[...] + jnp.dot(vt_ref[j], p, preferred_element_type=F32)
        m_ref[...] = m_new

    def body(j, carry):
        chunk(j, False, True)
        return carry

    n_full = qi * per_q
    s_ref[0] = scores(0)
    lax.fori_loop(0, n_full, body, 0)
    for d in range(per_q):
        chunk(n_full + d, True, d + 1 < per_q)
    out_t = acc_ref[:HEAD_DIM, :] / acc_ref[HEAD_DIM:HEAD_DIM + 1, :]
    o_ref[...] = out_t.T.astype(o_ref.dtype)


def _mla_attention(q, kv, kr, *, batch, seq, tq, tk):
    n = q.shape[0]
    nq = seq // tq
    return pl.pallas_call(
        functools.partial(_mla_kernel, tq=tq, tk=tk),
        grid=(batch, MLA_HEADS, nq),
        in_specs=[
            pl.BlockSpec((tq, 2 * LANES), lambda b, h, i: (b * nq + i, h)),
            pl.BlockSpec((seq, HEAD_DIM), lambda b, h, i: (b, h)),
            pl.BlockSpec((seq, LANES), lambda b, h, i: (b, 0)),
            pl.BlockSpec((seq, HEAD_DIM), lambda b, h, i: (b, MLA_HEADS + h)),
        ],
        out_specs=pl.BlockSpec((tq, HEAD_DIM), lambda b, h, i: (b * nq + i, h)),
        out_shape=jax.ShapeDtypeStruct((n, MLA_HEADS * HEAD_DIM), BF16),
        scratch_shapes=[
            pltpu.VMEM((seq, 2 * LANES), BF16),
            pltpu.VMEM((seq // tk, HEAD_DIM + BF16_SUBLANES, tk), BF16),
            pltpu.VMEM((2 * LANES, tq), BF16),
            pltpu.VMEM((1, tq), F32),
            pltpu.VMEM((HEAD_DIM + BF16_SUBLANES, tq), F32),
            pltpu.VMEM((2, tk, tq), F32),
        ],
        compiler_params=_cparams(("arbitrary", "arbitrary", "arbitrary")),
        name="mla_attention",
    )(q, kv, kr, kv)


def _dil_kernel(q_ref, kp_ref, kc_ref, vp_ref, vc_ref, acc_ref, st_ref, *, tq):
    ti = pl.program_id(2)
    nk = tq + DIL_KEYS
    qpos = lax.broadcasted_iota(jnp.int32, (tq, nk), 0)
    kpos = lax.broadcasted_iota(jnp.int32, (tq, nk), 1)
    dist = qpos + DIL_KEYS - kpos
    valid = (dist >= 0) & (dist <= DIL_KEYS) & ((kpos >= DIL_KEYS) | (ti > 0))
    st_ref[...] = jnp.zeros(st_ref.shape, F32)
    for h in range(DIL_HEADS):
        sl = slice(h * HEAD_DIM, (h + 1) * HEAD_DIM)
        q = q_ref[0, :, sl]
        k = jnp.concatenate([kp_ref[0, :, sl], kc_ref[0, :, sl]], axis=0)
        v = jnp.concatenate([vp_ref[0, :, sl], vc_ref[0, :, sl]], axis=0)
        s = lax.dot_general(q, k, (((1,), (1,)), ((), ())), preferred_element_type=F32)
        s = jnp.where(valid, s, NEG_INF)
        m = jnp.max(s, axis=-1, keepdims=True)
        e = jnp.exp(s - m)
        l = jnp.sum(e, axis=-1, keepdims=True)
        acc_ref[0, :, sl] = jnp.dot(e.astype(BF16), v, preferred_element_type=F32)
        st_ref[0, :, h:h + 1] = m
        st_ref[0, :, DIL_HEADS + h:DIL_HEADS + h + 1] = l


def _dilated(qkv, *, batch, seq, dil, tq):
    t = seq // dil
    tq = min(tq, t)
    nt = t // tq
    cols = DIL_HEADS * HEAD_DIM
    view = qkv.reshape(batch, t, dil * 3 * cols)
    sub = tq // DIL_KEYS

    def prev(b, r, i):
        return (b, jnp.maximum(i * sub - 1, 0), 3 * r + 1)

    def prev_v(b, r, i):
        return (b, jnp.maximum(i * sub - 1, 0), 3 * r + 2)

    acc, st = pl.pallas_call(
        functools.partial(_dil_kernel, tq=tq),
        grid=(batch, dil, nt),
        in_specs=[
            pl.BlockSpec((1, tq, cols), lambda b, r, i: (b, i, 3 * r)),
            pl.BlockSpec((1, DIL_KEYS, cols), prev),
            pl.BlockSpec((1, tq, cols), lambda b, r, i: (b, i, 3 * r + 1)),
            pl.BlockSpec((1, DIL_KEYS, cols), prev_v),
            pl.BlockSpec((1, tq, cols), lambda b, r, i: (b, i, 3 * r + 2)),
        ],
        out_specs=[
            pl.BlockSpec((1, tq, cols), lambda b, r, i: (b, i, r)),
            pl.BlockSpec((1, tq, LANES), lambda b, r, i: (b, i, r)),
        ],
        out_shape=[jax.ShapeDtypeStruct((batch, t, dil * cols), F32),
                   jax.ShapeDtypeStruct((batch, t, dil * LANES), F32)],
        compiler_params=_cparams(("arbitrary", "arbitrary", "arbitrary")),
        name=f"dilated_d{dil}",
    )(view, view, view, view, view)
    return acc.reshape(batch * seq, cols), st.reshape(batch * seq, LANES)


def _out_proj_kernel(oa_ref, a1_ref, a2_ref, a3_ref, s1_ref, s2_ref, s3_ref, ga_ref, gb_ref,
                     w_ref, x_ref, o_ref, h_ref, ob_ref):
    j = pl.program_id(1)
    half = oa_ref.shape[1]

    @pl.when(j == 0)
    def _():
        stats = (s1_ref[...], s2_ref[...], s3_ref[...])
        head_lane = lax.broadcasted_iota(jnp.int32, stats[0].shape, 1) < DIL_HEADS
        m_all = jnp.maximum(jnp.maximum(stats[0], stats[1]), stats[2])
        ws = [jnp.exp(s - m_all) for s in stats]
        ls = [pltpu.roll(s, LANES - DIL_HEADS, 1) for s in stats]
        den = ws[0] * ls[0] + ws[1] * ls[1] + ws[2] * ls[2]
        us = [w / jnp.where(head_lane, den, 1.0) for w in ws]
        for h in range(DIL_HEADS):
            sl = slice(h * HEAD_DIM, (h + 1) * HEAD_DIM)
            ob_ref[:, sl] = (us[0][:, h:h + 1] * a1_ref[:, sl] + us[1][:, h:h + 1] * a2_ref[:, sl]
                             + us[2][:, h:h + 1] * a3_ref[:, sl])
        h_ref[:, :half] = _rms(oa_ref[...].astype(F32), ga_ref[...]).astype(BF16)
        h_ref[:, half:] = _rms(ob_ref[...], gb_ref[...]).astype(BF16)

    o_ref[...] = x_ref[...] + jnp.dot(h_ref[...], w_ref[...], preferred_element_type=F32)


def _out_proj(o_a, accs, stats, g_a, g_b, w, x, *, tm, tn):
    n, d = x.shape
    half = o_a.shape[1]
    row = lambda i, j: (i, 0)
    return pl.pallas_call(
        _out_proj_kernel,
        grid=(n // tm, d // tn),
        in_specs=[pl.BlockSpec((tm, half), row)]
        + [pl.BlockSpec((tm, half), row)] * 3
        + [pl.BlockSpec((tm, LANES), row)] * 3
        + [pl.BlockSpec((1, half), lambda i, j: (0, 0))] * 2
        + [pl.BlockSpec((2 * half, tn), lambda i, j: (0, j)),
           pl.BlockSpec((tm, tn), lambda i, j: (i, j))],
        out_specs=pl.BlockSpec((tm, tn), lambda i, j: (i, j)),
        out_shape=jax.ShapeDtypeStruct((n, d), F32),
        scratch_shapes=[pltpu.VMEM((tm, 2 * half), BF16), pltpu.VMEM((tm, half), F32)],
        compiler_params=_cparams(("arbitrary", "arbitrary")),
        name="out_proj",
    )(o_a, *accs, *stats, g_a.reshape(1, -1), g_b.reshape(1, -1), w, x)


def _gate_up_kernel(x_ref, g_ref, wg_ref, wu_ref, o_ref, h_ref):
    @pl.when(pl.program_id(1) == 0)
    def _():
        h_ref[...] = _rms(x_ref[...], g_ref[...]).astype(BF16)

    h = h_ref[...]
    gate = jnp.dot(h, wg_ref[...], preferred_element_type=F32)
    up = jnp.dot(h, wu_ref[...], preferred_element_type=F32)
    o_ref[...] = (gate * jax.nn.sigmoid(gate) * up).astype(o_ref.dtype)


def _gate_up(x, g, wg, wu, *, tm, tn):
    n, d = x.shape
    f = wg.shape[1]
    return pl.pallas_call(
        _gate_up_kernel,
        grid=(n // tm, f // tn),
        in_specs=[
            pl.BlockSpec((tm, d), lambda i, j: (i, 0)),
            pl.BlockSpec((1, d), lambda i, j: (0, 0)),
            pl.BlockSpec((d, tn), lambda i, j: (0, j)),
            pl.BlockSpec((d, tn), lambda i, j: (0, j)),
        ],
        out_specs=pl.BlockSpec((tm, tn), lambda i, j: (i, j)),
        out_shape=jax.ShapeDtypeStruct((n, f), BF16),
        scratch_shapes=[pltpu.VMEM((tm, d), BF16)],
        compiler_params=_cparams(("arbitrary", "arbitrary")),
        name="ffn_gate_up",
    )(x, g.reshape(1, -1), wg, wu)


def _down_kernel(a_ref, w_ref, x_ref, o_ref):
    o_ref[...] = x_ref[...] + jnp.dot(a_ref[...], w_ref[...], preferred_element_type=F32)


def _down(act, w, x, *, tm, tn):
    n, f = act.shape
    d = w.shape[1]
    return pl.pallas_call(
        _down_kernel,
        grid=(n // tm, d // tn),
        in_specs=[
            pl.BlockSpec((tm, f), lambda i, j: (i, 0)),
            pl.BlockSpec((f, tn), lambda i, j: (0, j)),
            pl.BlockSpec((tm, tn), lambda i, j: (i, j)),
        ],
        out_specs=pl.BlockSpec((tm, tn), lambda i, j: (i, j)),
        out_shape=jax.ShapeDtypeStruct((n, d), F32),
        compiler_params=_cparams(("arbitrary", "arbitrary")),
        name="ffn_down",
    )(act, w, x)


def _router_kernel(x_ref, g_ref, w_ref, idx_ref, wt_ref):
    h = _rms(x_ref[...], g_ref[...])
    logits = jnp.dot(h, w_ref[...], preferred_element_type=F32,
                     precision=lax.Precision.HIGHEST)
    lane_i = lax.broadcasted_iota(jnp.int32, logits.shape, 1)
    lane = lane_i.astype(F32)
    logits = jnp.where(lane_i < N_EXPERTS, logits, -jnp.inf)
    v1 = jnp.max(logits, axis=-1, keepdims=True)
    i1 = jnp.min(jnp.where(logits == v1, lane, float(LANES)), axis=-1, keepdims=True)
    rest = jnp.where(lane == i1, -jnp.inf, logits)
    v2 = jnp.max(rest, axis=-1, keepdims=True)
    i2 = jnp.min(jnp.where(rest == v2, lane, float(LANES)), axis=-1, keepdims=True)
    e2 = jnp.exp(v2 - v1)
    w1 = 1.0 / (1.0 + e2)
    w2 = e2 / (1.0 + e2)
    idx_ref[...] = jnp.where(lane_i == 0, i1, jnp.where(lane_i == 1, i2, 0.0)).astype(jnp.int32)
    wt_ref[...] = jnp.where(lane_i == 0, w1, jnp.where(lane_i == 1, w2, 0.0))


def _router(x, g, w_pad, *, tm):
    n, d = x.shape
    return pl.pallas_call(
        _router_kernel,
        grid=(n // tm,),
        in_specs=[
            pl.BlockSpec((tm, d), lambda i: (i, 0)),
            pl.BlockSpec((1, d), lambda i: (0, 0)),
            pl.BlockSpec((d, LANES), lambda i: (0, 0)),
        ],
        out_specs=[pl.BlockSpec((tm, LANES), lambda i: (i, 0))] * 2,
        out_shape=[jax.ShapeDtypeStruct((n, LANES), jnp.int32),
                   jax.ShapeDtypeStruct((n, LANES), F32)],
        compiler_params=_cparams(("arbitrary",)),
        name="moe_router",
    )(x, g.reshape(1, -1), w_pad)


def _row_copy(src_hbm, row, dst_vmem, slot, sem):
    return pltpu.make_async_copy(src_hbm.at[pl.ds(row, 1)], dst_vmem.at[pl.ds(slot, 1)], sem)


DMA_UNROLL = 8


def _load_indices(idx_hbm, tile, idx_smem, isem):
    cp = pltpu.make_async_copy(idx_hbm.at[tile], idx_smem, isem)
    cp.start()
    cp.wait()


def _start_rows(idx_smem, src_hbm, dst_vmem, sem):
    def start(i, c):
        _row_copy(src_hbm, idx_smem[i], dst_vmem, i, sem).start()
        return c

    lax.fori_loop(0, dst_vmem.shape[0], start, 0, unroll=DMA_UNROLL)


def _wait_rows(src_hbm, dst_vmem, sem):
    def wait(i, c):
        _row_copy(src_hbm, 0, dst_vmem, i, sem).wait()
        return c

    lax.fori_loop(0, dst_vmem.shape[0], wait, 0, unroll=DMA_UNROLL)


def _gather_norm_kernel(idx_hbm, x_hbm, g_ref, o_ref, idx_smem, buf, isem, sem):
    i = pl.program_id(0)
    n = pl.num_programs(0)

    def fetch(tile, slot):
        _load_indices(idx_hbm, tile, idx_smem, isem)
        _start_rows(idx_smem, x_hbm, buf.at[slot], sem.at[slot])

    @pl.when(i == 0)
    def _():
        fetch(0, 0)

    @pl.when(i + 1 < n)
    def _():
        fetch(i + 1, (i + 1) % 2)

    slot = i % 2
    _wait_rows(x_hbm, buf.at[slot], sem.at[slot])
    o_ref[...] = _rms(buf[slot], g_ref[...]).astype(o_ref.dtype)


def _gather_norm(row_token, x, g, *, tr):
    nt = row_token.shape[0]
    d = x.shape[1]
    return pl.pallas_call(
        _gather_norm_kernel,
        grid=(nt,),
        in_specs=[
            pl.BlockSpec(memory_space=pl.ANY),
            pl.BlockSpec(memory_space=pl.ANY),
            pl.BlockSpec((1, d), lambda i: (0, 0)),
        ],
        out_specs=pl.BlockSpec((tr, d), lambda i: (i, 0)),
        out_shape=jax.ShapeDtypeStruct((nt * tr, d), BF16),
        scratch_shapes=[
            pltpu.SMEM((tr,), jnp.int32),
            pltpu.VMEM((2, tr, d), F32),
            pltpu.SemaphoreType.DMA,
            pltpu.SemaphoreType.DMA((2,)),
        ],
        compiler_params=_cparams(("arbitrary",)),
        name="moe_gather",
    )(row_token, x, g.reshape(1, -1))


def _moe_gate_up_kernel(te_ref, nu_ref, x_ref, wg_ref, wu_ref, o_ref):
    i = pl.program_id(1)

    @pl.when(i < nu_ref[0])
    def _():
        h = x_ref[...]
        gate = jnp.dot(h, wg_ref[0], preferred_element_type=F32)
        up = jnp.dot(h, wu_ref[0], preferred_element_type=F32)
        o_ref[...] = (gate * jax.nn.sigmoid(gate) * up).astype(o_ref.dtype)

    @pl.when(i >= nu_ref[0])
    def _():
        o_ref[...] = jnp.zeros(o_ref.shape, o_ref.dtype)


def _moe_gate_up(tile_expert, n_used, xs, wg, wu, *, tm, tn):
    r, d = xs.shape
    f = wg.shape[2]
    grid_spec = pltpu.PrefetchScalarGridSpec(
        num_scalar_prefetch=2,
        grid=(f // tn, r // tm),
        in_specs=[
            pl.BlockSpec((tm, d), lambda j, i, te, nu: (i, 0)),
            pl.BlockSpec((1, d, tn), lambda j, i, te, nu: (te[i], 0, j)),
            pl.BlockSpec((1, d, tn), lambda j, i, te, nu: (te[i], 0, j)),
        ],
        out_specs=pl.BlockSpec((tm, tn), lambda j, i, te, nu: (i, j)),
    )
    return pl.pallas_call(
        _moe_gate_up_kernel,
        grid_spec=grid_spec,
        out_shape=jax.ShapeDtypeStruct((r, f), BF16),
        compiler_params=_cparams(("arbitrary", "arbitrary")),
        name="moe_gate_up",
    )(tile_expert, n_used, xs, wg, wu)


def _moe_down_kernel(te_ref, nu_ref, a_ref, w_ref, rw_ref, o_ref):
    i = pl.program_id(1)

    @pl.when(i < nu_ref[0])
    def _():
        y = jnp.dot(a_ref[...], w_ref[0], preferred_element_type=F32)
        o_ref[...] = y * rw_ref[...]

    @pl.when(i >= nu_ref[0])
    def _():
        o_ref[...] = jnp.zeros(o_ref.shape, o_ref.dtype)


def _moe_down(tile_expert, n_used, act, w, row_w, *, tm, tn):
    r, f = act.shape
    d = w.shape[2]
    grid_spec = pltpu.PrefetchScalarGridSpec(
        num_scalar_prefetch=2,
        grid=(d // tn, r // tm),
        in_specs=[
            pl.BlockSpec((tm, f), lambda j, i, te, nu: (i, 0)),
            pl.BlockSpec((1, f, tn), lambda j, i, te, nu: (te[i], 0, j)),
            pl.BlockSpec((tm, 1), lambda j, i, te, nu: (i, 0)),
        ],
        out_specs=pl.BlockSpec((tm, tn), lambda j, i, te, nu: (i, j)),
    )
    return pl.pallas_call(
        _moe_down_kernel,
        grid_spec=grid_spec,
        out_shape=jax.ShapeDtypeStruct((r, d), F32),
        compiler_params=_cparams(("arbitrary", "arbitrary")),
        name="moe_down",
    )(tile_expert, n_used, act, w, row_w)


def _combine_kernel(p0_hbm, p1_hbm, ys_hbm, x_ref, g_ref, o_ref, i0_smem, i1_smem, b0, b1,
                    isem, sem0, sem1, *, final_norm):
    i = pl.program_id(0)
    n = pl.num_programs(0)

    def fetch(tile, slot):
        _load_indices(p0_hbm, tile, i0_smem, isem)
        _load_indices(p1_hbm, tile, i1_smem, isem)
        _start_rows(i0_smem, ys_hbm, b0.at[slot], sem0.at[slot])
        _start_rows(i1_smem, ys_hbm, b1.at[slot], sem1.at[slot])

    @pl.when(i == 0)
    def _():
        fetch(0, 0)

    @pl.when(i + 1 < n)
    def _():
        fetch(i + 1, (i + 1) % 2)

    slot = i % 2
    _wait_rows(ys_hbm, b0.at[slot], sem0.at[slot])
    _wait_rows(ys_hbm, b1.at[slot], sem1.at[slot])
    y = x_ref[...] + (b0[slot] + b1[slot])
    if final_norm:
        y = _rms(y, g_ref[...])
    o_ref[...] = y


def _combine(pos0, pos1, ys, x, g, *, tr, final_norm):
    n, d = x.shape
    return pl.pallas_call(
        functools.partial(_combine_kernel, final_norm=final_norm),
        grid=(n // tr,),
        in_specs=[
            pl.BlockSpec(memory_space=pl.ANY),
            pl.BlockSpec(memory_space=pl.ANY),
            pl.BlockSpec(memory_space=pl.ANY),
            pl.BlockSpec((tr, d), lambda i: (i, 0)),
            pl.BlockSpec((1, d), lambda i: (0, 0)),
        ],
        out_specs=pl.BlockSpec((tr, d), lambda i: (i, 0)),
        out_shape=jax.ShapeDtypeStruct((n, d), F32),
        scratch_shapes=[
            pltpu.SMEM((tr,), jnp.int32),
            pltpu.SMEM((tr,), jnp.int32),
            pltpu.VMEM((2, tr, d), F32),
            pltpu.VMEM((2, tr, d), F32),
            pltpu.SemaphoreType.DMA,
            pltpu.SemaphoreType.DMA((2,)),
            pltpu.SemaphoreType.DMA((2,)),
        ],
        compiler_params=_cparams(("arbitrary",)),
        name="moe_combine",
    )(pos0.reshape(n // tr, tr), pos1.reshape(n // tr, tr), ys, x, g.reshape(1, -1))


def _rmsnorm_kernel(x_ref, g_ref, o_ref):
    o_ref[...] = _rms(x_ref[...], g_ref[...])


def _final_norm(x, g, *, tm):
    n, d = x.shape
    return pl.pallas_call(
        _rmsnorm_kernel,
        grid=(n // tm,),
        in_specs=[pl.BlockSpec((tm, d), lambda i: (i, 0)),
                  pl.BlockSpec((1, d), lambda i: (0, 0))],
        out_specs=pl.BlockSpec((tm, d), lambda i: (i, 0)),
        out_shape=jax.ShapeDtypeStruct((n, d), F32),
        compiler_params=_cparams(("arbitrary",)),
        name="final_norm",
    )(x, g.reshape(1, -1))


def _routing_plan(top_idx, top_w, *, tm):
    n = top_idx.shape[0]
    n_assign = 2 * n
    rows = n_assign + N_EXPERTS * tm
    n_tiles = rows // tm
    e_flat = top_idx.reshape(n_assign)
    onehot = (e_flat[:, None] == jnp.arange(N_EXPERTS, dtype=jnp.int32)[None, :]).astype(jnp.int32)
    counts = jnp.sum(onehot, axis=0)
    rank = jnp.sum((jnp.cumsum(onehot, axis=0) - onehot) * onehot, axis=1)
    padded = ((counts + tm - 1) // tm) * tm
    ends = jnp.cumsum(padded)
    starts = ends - padded
    dest = starts[e_flat] + rank
    token = jnp.arange(n_assign, dtype=jnp.int32) // 2
    row_token = jnp.zeros((rows,), jnp.int32).at[dest].set(token)
    row_w = jnp.zeros((rows,), F32).at[dest].set(top_w.reshape(n_assign))
    tile_start = jnp.arange(n_tiles, dtype=jnp.int32) * tm
    tile_expert = jnp.sum((tile_start[:, None] >= ends[None, :]).astype(jnp.int32), axis=1)
    tile_expert = jnp.minimum(tile_expert, N_EXPERTS - 1).astype(jnp.int32)
    n_used = (ends[-1] // tm).astype(jnp.int32).reshape(1)
    pos = dest.reshape(n, 2).astype(jnp.int32)
    return row_token.reshape(n_tiles, tm), row_w.reshape(rows, 1), tile_expert, n_used, pos


def _moe_layer(x, g, router_pad, wg, wu, wd, final_g, *, final_norm):
    tm = 512
    idx, wt = _router(x, g, router_pad, tm=1024)
    row_token, row_w, tile_expert, n_used, pos = _routing_plan(idx[:, :2], wt[:, :2], tm=tm)
    xs = _gather_norm(row_token, x, g, tr=tm)
    act = _moe_gate_up(tile_expert, n_used, xs, wg, wu, tm=tm, tn=1024)
    ys = _moe_down(tile_expert, n_used, act, wd, row_w, tm=tm, tn=512)
    return _combine(pos[:, 0], pos[:, 1], ys, x, final_g, tr=256, final_norm=final_norm)


def _rope_tables(seq):
    pos = jnp.arange(seq, dtype=F32)[:, None]

    def tab(dim):
        inv = ROPE_THETA ** (-jnp.arange(0, dim, 2, dtype=F32) / dim)
        ang = pos * inv[None, :]
        return jnp.cos(ang), jnp.sin(ang)

    cf, sf = tab(HEAD_DIM)
    cos_f = jnp.concatenate([cf, cf], axis=1)
    sin_f = jnp.concatenate([-sf, sf], axis=1)
    cr, sr = tab(ROPE_DIM)
    zero = jnp.zeros((seq, LANES - ROPE_DIM), F32)
    cos_r = jnp.concatenate([cr, cr, zero], axis=1)
    sin_r = jnp.concatenate([-sr, sr, zero], axis=1)
    return cos_f, sin_f, cos_r, sin_r


def kernel(x, attn_norm_g, w_in, q_norm_g, kv_norm_g, w_uq, w_uk, w_uv, out_a_norm_g,
           out_b_norm_g, w_out, ffn_norm_g, dense_w_gate, dense_w_up, dense_w_down,
           router_w, moe_w_gate, moe_w_up, moe_w_down, final_norm_g):
    batch, seq, d_model = x.shape
    depth = w_in.shape[0]
    n = batch * seq
    mix = DIL_HEADS * HEAD_DIM
    cos_f, sin_f, cos_r, sin_r = _rope_tables(seq)

    c0, c1, c2 = Q_LORA, Q_LORA + KV_LORA, Q_LORA + KV_LORA + ROPE_DIM
    pad = jnp.zeros((depth, d_model, 1024 - Q_LORA - ROPE_DIM), F32)
    w_lat = jnp.concatenate([w_in[:, :, :c0], w_in[:, :, c1:c2], pad, w_in[:, :, c0:c1]],
                            axis=2).astype(BF16)
    w_qkv = w_in[:, :, c2:].astype(BF16)
    uq = w_uq.reshape(depth, Q_LORA, MLA_HEADS, QK_DIM)
    uq = jnp.concatenate([uq, jnp.zeros((depth, Q_LORA, MLA_HEADS, 2 * LANES - QK_DIM), F32)],
                         axis=3)
    w_uq_p = uq.reshape(depth, Q_LORA, MLA_HEADS * 2 * LANES).astype(BF16)
    w_ukv = jnp.concatenate([w_uk, w_uv], axis=2).astype(BF16)
    w_out_b = w_out.astype(BF16)
    dense_g = dense_w_gate.astype(BF16)
    dense_u = dense_w_up.astype(BF16)
    dense_d = dense_w_down.astype(BF16)
    moe_g = moe_w_gate.astype(BF16)
    moe_u = moe_w_up.astype(BF16)
    moe_d = moe_w_down.astype(BF16)
    router_pad = jnp.concatenate(
        [router_w, jnp.zeros(router_w.shape[:2] + (LANES - N_EXPERTS,), F32)], axis=2)

    xf = x.reshape(n, d_model)
    for layer in range(depth):
        g_attn = attn_norm_g[layer]
        lat = _norm_mm(xf, g_attn, w_lat[layer], cos_f, sin_f, x_block=(d_model, 0),
                       k_norm=d_model, tm=1024, tn=512, name="in_proj_latent")
        qkv = _norm_mm(xf, g_attn, w_qkv[layer], cos_f, sin_f, x_block=(d_model, 0),
                       k_norm=d_model, tm=1024, tn=512, rope_tiles=2 * mix // 512,
                       scale_tiles=mix // 512, scale=HEAD_DIM ** -0.5, name="in_proj_qkv")
        q, kr = _q_up(lat, q_norm_g[layer], w_uq_p[layer], cos_r, sin_r, tm=1024, tn=512,
                      scale=LOG2_E * QK_DIM ** -0.5)
        kv = _norm_mm(lat, kv_norm_g[layer], w_ukv[layer], cos_f, sin_f,
                      x_block=(KV_LORA, 1024 // KV_LORA), k_norm=KV_LORA, tm=1024, tn=1024,
                      name="kv_up")
        o_a = _mla_attention(q, kv, kr, batch=batch, seq=seq, tq=1024, tk=512)
        accs, stats = [], []
        for _, dil in DIL_CONFIGS:
            a, s = _dilated(qkv, batch=batch, seq=seq, dil=dil, tq=256)
            accs.append(a)
            stats.append(s)
        xf = _out_proj(o_a, accs, stats, out_a_norm_g[layer], out_b_norm_g[layer],
                       w_out_b[layer], xf, tm=512, tn=512)
        j = layer // 2
        last = layer == depth - 1
        if layer % 2 == 0:
            act = _gate_up(xf, ffn_norm_g[layer], dense_g[j], dense_u[j], tm=1024, tn=512)
            xf = _down(act, dense_d[j], xf, tm=1024, tn=512)
            if last:
                xf = _final_norm(xf, final_norm_g, tm=1024)
        else:
            xf = _moe_layer(xf, ffn_norm_g[layer], router_pad[j], moe_g[j], moe_u[j], moe_d[j],
                            final_norm_g, final_norm=last)
    return xf.reshape(batch, seq, d_model)
```

```python
import functools

import jax
import jax.numpy as jnp
from jax import lax
from jax.experimental import pallas as pl
from jax.experimental.pallas import tpu as pltpu

F32 = jnp.float32
BF16 = jnp.bfloat16

LANES = 128
BF16_SUBLANES = 16
MLA_HEADS = 8
DIL_HEADS = 8
HEAD_DIM = 128
Q_LORA = 768
KV_LORA = 512
ROPE_DIM = 64
QK_DIM = HEAD_DIM + ROPE_DIM
N_EXPERTS = 8
DIL_CONFIGS = ((128, 1), (512, 4), (2048, 16))
DIL_PAD = max(w for w, _ in DIL_CONFIGS)
DIL_TQ = 512
LATENT_COLS = 1536
PROJ_TN = 512
ROPE_THETA = 10000.0
EPS = 1e-6
NEG_INF = -1e30
LOG2_E = 1.4426950408889634
VMEM_LIMIT = 56 * 1024 * 1024


def _cparams(sem):
    return pltpu.CompilerParams(dimension_semantics=sem, vmem_limit_bytes=VMEM_LIMIT)


def _rms(xf, g):
    ms = jnp.mean(xf * xf, axis=-1, keepdims=True)
    return xf * lax.rsqrt(ms + EPS) * g


def _rope_full(a, c, s):
    return a * c + pltpu.roll(a, HEAD_DIM // 2, 1) * s


def _rope_half(a, c, s):
    lane = lax.broadcasted_iota(jnp.int32, a.shape, 1)
    lower = (lane % ROPE_DIM) < (ROPE_DIM // 2)
    partner = jnp.where(lower, pltpu.roll(a, LANES - ROPE_DIM // 2, 1),
                        pltpu.roll(a, ROPE_DIM // 2, 1))
    return a * c + partner * s


def _norm_mm_kernel(x_ref, g_ref, w_ref, cos_ref, sin_ref, o_ref, h_ref, *,
                    k_norm, rope_tiles, scale_tiles, scale):
    j = pl.program_id(1)

    @pl.when(j == 0)
    def _():
        xf = x_ref[:, :k_norm].astype(F32)
        h_ref[...] = _rms(xf, g_ref[...]).astype(BF16)

    acc = jnp.dot(h_ref[...], w_ref[...], preferred_element_type=F32)
    if rope_tiles is None:
        o_ref[...] = acc.astype(o_ref.dtype)
        return
    rope = (j >= rope_tiles[0]) & (j < rope_tiles[1])

    @pl.when(rope)
    def _():
        scaled = j < scale_tiles
        c = jnp.where(scaled, cos_ref[...] * scale, cos_ref[...])
        s = jnp.where(scaled, sin_ref[...] * scale, sin_ref[...])
        for k in range(acc.shape[1] // HEAD_DIM):
            sl = slice(k * HEAD_DIM, (k + 1) * HEAD_DIM)
            o_ref[:, sl] = _rope_full(acc[:, sl], c, s).astype(o_ref.dtype)

    @pl.when(jnp.logical_not(rope))
    def _():
        o_ref[...] = acc.astype(o_ref.dtype)


def _norm_mm(x, g, w, cos, sin, *, x_block, k_norm, tm, tn, rope_tiles=None, scale_tiles=0,
             scale=1.0, name):
    n = x.shape[0]
    xw, xb = x_block
    kw, nout = w.shape
    assert kw == k_norm and n % tm == 0 and nout % tn == 0
    ns = cos.shape[0] // tm
    kern = functools.partial(_norm_mm_kernel, k_norm=k_norm, rope_tiles=rope_tiles,
                             scale_tiles=scale_tiles, scale=scale)
    return pl.pallas_call(
        kern,
        grid=(n // tm, nout // tn),
        in_specs=[
            pl.BlockSpec((tm, xw), lambda i, j: (i, xb)),
            pl.BlockSpec((1, k_norm), lambda i, j: (0, 0)),
            pl.BlockSpec((k_norm, tn), lambda i, j: (0, j)),
            pl.BlockSpec((tm, HEAD_DIM), lambda i, j: (i % ns, 0)),
            pl.BlockSpec((tm, HEAD_DIM), lambda i, j: (i % ns, 0)),
        ],
        out_specs=pl.BlockSpec((tm, tn), lambda i, j: (i, j)),
        out_shape=jax.ShapeDtypeStruct((n, nout), BF16),
        scratch_shapes=[pltpu.VMEM((tm, k_norm), BF16)],
        compiler_params=_cparams(("arbitrary", "arbitrary")),
        name=name,
    )(x, g.reshape(1, -1), w, cos, sin)


def _q_up_kernel(x_ref, g_ref, w_ref, cos_ref, sin_ref, q_ref, kr_ref, h_ref, *, scale):
    j = pl.program_id(1)
    c = cos_ref[...]
    s = sin_ref[...]

    @pl.when(j == 0)
    def _():
        blk = x_ref[...].astype(F32)
        h_ref[...] = _rms(blk[:, :Q_LORA], g_ref[...]).astype(BF16)
        kr_ref[...] = _rope_half(blk[:, Q_LORA:Q_LORA + LANES], c, s).astype(BF16)

    acc = jnp.dot(h_ref[...], w_ref[...], preferred_element_type=F32)
    for hh in range(acc.shape[1] // (2 * LANES)):
        lo = hh * 2 * LANES
        q_ref[:, lo:lo + LANES] = (acc[:, lo:lo + LANES] * scale).astype(BF16)
        q_ref[:, lo + LANES:lo + 2 * LANES] = (
            _rope_half(acc[:, lo + LANES:lo + 2 * LANES], c, s) * scale).astype(BF16)


def _q_up(lat, g, w, cos, sin, *, tm, tn, scale):
    n = lat.shape[0]
    nout = w.shape[1]
    ns = cos.shape[0] // tm
    return pl.pallas_call(
        functools.partial(_q_up_kernel, scale=scale),
        grid=(n // tm, nout // tn),
        in_specs=[
            pl.BlockSpec((tm, 1024), lambda i, j: (i, 0)),
            pl.BlockSpec((1, Q_LORA), lambda i, j: (0, 0)),
            pl.BlockSpec((Q_LORA, tn), lambda i, j: (0, j)),
            pl.BlockSpec((tm, LANES), lambda i, j: (i % ns, 0)),
            pl.BlockSpec((tm, LANES), lambda i, j: (i % ns, 0)),
        ],
        out_specs=[
            pl.BlockSpec((tm, tn), lambda i, j: (i, j)),
            pl.BlockSpec((tm, LANES), lambda i, j: (i, 0)),
        ],
        out_shape=[jax.ShapeDtypeStruct((n, nout), BF16),
                   jax.ShapeDtypeStruct((n, LANES), BF16)],
        scratch_shapes=[pltpu.VMEM((tm, Q_LORA), BF16)],
        compiler_params=_cparams(("arbitrary", "arbitrary")),
        name="q_up",
    )(lat, g.reshape(1, -1), w, cos, sin)


def _mla_kernel(q_ref, kn_ref, kr_ref, v_ref, o_ref, kcat_ref, vt_ref, qt_ref, m_ref,
                acc_ref, s_ref, *, tq, tk):
    qi = pl.program_id(2)
    seq = kcat_ref.shape[0]
    vt_rows = vt_ref.shape[1]

    @pl.when(qi == 0)
    def _():
        kcat_ref[:, :HEAD_DIM] = kn_ref[...]
        kcat_ref[:, HEAD_DIM:] = kr_ref[...]
        ones_row = (lax.broadcasted_iota(jnp.int32, (vt_rows - HEAD_DIM, tk), 0) == 0)

        def transpose_chunk(c, carry):
            off = pl.multiple_of(c * tk, tk)
            vt_ref[c, :HEAD_DIM, :] = v_ref[pl.ds(off, tk), :].astype(F32).T.astype(BF16)
            vt_ref[c, HEAD_DIM:, :] = ones_row.astype(BF16)
            return carry

        lax.fori_loop(0, seq // tk, transpose_chunk, 0)

    qt_ref[...] = q_ref[...].astype(F32).T.astype(BF16)
    m_ref[...] = jnp.full(m_ref.shape, NEG_INF, F32)
    acc_ref[...] = jnp.zeros(acc_ref.shape, F32)
    per_q = tq // tk

    def scores(j):
        off = pl.multiple_of(j * tk, tk)
        return jnp.dot(kcat_ref[pl.ds(off, tk), :], qt_ref[...],
                       preferred_element_type=F32)

    def chunk(j, masked, has_next):
        s = s_ref[j % 2]
        if has_next:
            s_ref[(j + 1) % 2] = scores(j + 1)
        if masked:
            key = j * tk + lax.broadcasted_iota(jnp.int32, (tk, tq), 0)
            qry = qi * tq + lax.broadcasted_iota(jnp.int32, (tk, tq), 1)
            s = jnp.where(key <= qry, s, NEG_INF)
        m_old = m_ref[...]
        m_new = jnp.maximum(m_old, jnp.max(s, axis=0, keepdims=True))
        a = jnp.exp2(m_old - m_new)
        p = jnp.exp2(s - m_new).astype(BF16)
        acc_ref[...] = a * acc_ref[...] + jnp.dot(vt_ref[j], p, preferred_element_type=F32)
        m_ref[...] = m_new

    def body(j, carry):
        chunk(j, False, True)
        return carry

    n_full = qi * per_q
    s_ref[0] = scores(0)
    lax.fori_loop(0, n_full, body, 0)
    for d in range(per_q):
        chunk(n_full + d, True, d + 1 < per_q)
    out_t = acc_ref[:HEAD_DIM, :] / acc_ref[HEAD_DIM:HEAD_DIM + 1, :]
    o_ref[...] = out_t.T.astype(o_ref.dtype)


def _mla_attention(q, kv, kr, *, batch, seq, tq, tk):
    n = q.shape[0]
    nq = seq // tq
    return pl.pallas_call(
        functools.partial(_mla_kernel, tq=tq, tk=tk),
        grid=(batch, MLA_HEADS, nq),
        in_specs=[
            pl.BlockSpec((tq, 2 * LANES), lambda b, h, i: (b * nq + i, h)),
            pl.BlockSpec((seq, HEAD_DIM), lambda b, h, i: (b, h)),
            pl.BlockSpec((seq, LANES), lambda b, h, i: (b, 0)),
            pl.BlockSpec((seq, HEAD_DIM), lambda b, h, i: (b, MLA_HEADS + h)),
        ],
        out_specs=pl.BlockSpec((tq, HEAD_DIM), lambda b, h, i: (b * nq + i, h)),
        out_shape=jax.ShapeDtypeStruct((n, MLA_HEADS * HEAD_DIM), BF16),
        scratch_shapes=[
            pltpu.VMEM((seq, 2 * LANES), BF16),
            pltpu.VMEM((seq // tk, HEAD_DIM + BF16_SUBLANES, tk), BF16),
            pltpu.VMEM((2 * LANES, tq), BF16),
            pltpu.VMEM((1, tq), F32),
            pltpu.VMEM((HEAD_DIM + BF16_SUBLANES, tq), F32),
            pltpu.VMEM((2, tk, tq), F32),
        ],
        compiler_params=_cparams(("arbitrary", "arbitrary", "arbitrary")),
        name="mla_attention",
    )(q, kv, kr, kv)


def _dil_bias(tq):
    kk = jnp.arange(DIL_PAD + tq, dtype=jnp.int32)[:, None]
    qq = jnp.arange(tq, dtype=jnp.int32)[None, :]
    delta = qq + DIL_PAD - kk
    count = jnp.zeros(delta.shape, F32)
    for win, dil in DIL_CONFIGS:
        count += ((delta >= 0) & (delta <= win) & (delta % dil == 0)).astype(F32)
    return jnp.where(count > 0, jnp.log2(jnp.maximum(count, 1.0)), NEG_INF)


def _dil_kernel(q_ref, k_ref, v_ref, bias_ref, o_ref, kpad_ref, vt_ref, s_ref, *, tq):
    i = pl.program_id(2)
    seq = k_ref.shape[0]
    ck = vt_ref.shape[2]
    vt_rows = vt_ref.shape[1]
    n_pad = DIL_PAD // ck
    win = DIL_PAD + tq

    @pl.when(i == 0)
    def _():
        kpad_ref[:DIL_PAD, :] = jnp.zeros((DIL_PAD, HEAD_DIM), BF16)
        kpad_ref[DIL_PAD:, :] = k_ref[...]
        ones_row = (lax.broadcasted_iota(jnp.int32, (vt_rows - HEAD_DIM, ck), 0) == 0)
        for c in range(n_pad):
            vt_ref[c] = jnp.zeros((vt_rows, ck), BF16)

        def transpose_chunk(c, carry):
            off = pl.multiple_of(c * ck, ck)
            vt_ref[n_pad + c, :HEAD_DIM, :] = v_ref[pl.ds(off, ck), :].astype(F32).T.astype(BF16)
            vt_ref[n_pad + c, HEAD_DIM:, :] = ones_row.astype(BF16)
            return carry

        lax.fori_loop(0, seq // ck, transpose_chunk, 0)

    s0 = pl.multiple_of(i * tq, tq)
    qt = q_ref[...].astype(F32).T.astype(BF16)
    s_ref[...] = jnp.dot(kpad_ref[pl.ds(s0, win), :], qt,
                         preferred_element_type=F32) + bias_ref[...]

    @pl.when(s0 < DIL_PAD)
    def _():
        row = lax.broadcasted_iota(jnp.int32, (win, tq), 0)
        s_ref[...] = jnp.where(row >= DIL_PAD - s0, s_ref[...], NEG_INF)

    s = s_ref[...]
    m = jnp.max(s, axis=0, keepdims=True)
    p = jnp.exp2(s - m).astype(BF16)
    base = i * (tq // ck)
    acc = jnp.zeros((vt_rows, tq), F32)
    for c in range(win // ck):
        acc += jnp.dot(vt_ref[base + c], p[c * ck:(c + 1) * ck, :], preferred_element_type=F32)
    o_ref[...] = (acc[:HEAD_DIM, :] / acc[HEAD_DIM:HEAD_DIM + 1, :]).T


def _dilated(qkv, bias, *, batch, seq, tq, ck):
    n = qkv.shape[0]
    nq = seq // tq
    win = DIL_PAD + tq
    vt_rows = HEAD_DIM + BF16_SUBLANES
    q0 = LATENT_COLS // HEAD_DIM
    return pl.pallas_call(
        functools.partial(_dil_kernel, tq=tq),
        grid=(batch, DIL_HEADS, nq),
        in_specs=[
            pl.BlockSpec((tq, HEAD_DIM), lambda b, h, i: (b * nq + i, q0 + h)),
            pl.BlockSpec((seq, HEAD_DIM), lambda b, h, i: (b, q0 + DIL_HEADS + h)),
            pl.BlockSpec((seq, HEAD_DIM), lambda b, h, i: (b, q0 + 2 * DIL_HEADS + h)),
            pl.BlockSpec((win, tq), lambda b, h, i: (0, 0)),
        ],
        out_specs=pl.BlockSpec((tq, HEAD_DIM), lambda b, h, i: (b * nq + i, h)),
        out_shape=jax.ShapeDtypeStruct((n, DIL_HEADS * HEAD_DIM), F32),
        scratch_shapes=[
            pltpu.VMEM((DIL_PAD + seq, HEAD_DIM), BF16),
            pltpu.VMEM(((DIL_PAD + seq) // ck, vt_rows, ck), BF16),
            pltpu.VMEM((win, tq), F32),
        ],
        compiler_params=_cparams(("arbitrary", "arbitrary", "arbitrary")),
        name="dilated_attention",
    )(qkv, qkv, qkv, bias)


def _out_proj_kernel(oa_ref, ob_ref, ga_ref, gb_ref, w_ref, x_ref, o_ref, h_ref):
    half = oa_ref.shape[1]

    @pl.when(pl.program_id(1) == 0)
    def _():
        h_ref[:, :half] = _rms(oa_ref[...].astype(F32), ga_ref[...]).astype(BF16)
        h_ref[:, half:] = _rms(ob_ref[...], gb_ref[...]).astype(BF16)

    o_ref[...] = x_ref[...] + jnp.dot(h_ref[...], w_ref[...], preferred_element_type=F32)


def _out_proj(o_a, o_b, g_a, g_b, w, x, *, tm, tn):
    n, d = x.shape
    half = o_a.shape[1]
    row = lambda i, j: (i, 0)
    return pl.pallas_call(
        _out_proj_kernel,
        grid=(n // tm, d // tn),
        in_specs=[pl.BlockSpec((tm, half), row)] * 2
        + [pl.BlockSpec((1, half), lambda i, j: (0, 0))] * 2
        + [pl.BlockSpec((2 * half, tn), lambda i, j: (0, j)),
           pl.BlockSpec((tm, tn), lambda i, j: (i, j))],
        out_specs=pl.BlockSpec((tm, tn), lambda i, j: (i, j)),
        out_shape=jax.ShapeDtypeStruct((n, d), F32),
        scratch_shapes=[pltpu.VMEM((tm, 2 * half), BF16)],
        compiler_params=_cparams(("arbitrary", "arbitrary")),
        name="out_proj",
    )(o_a, o_b, g_a.reshape(1, -1), g_b.reshape(1, -1), w, x)


def _gate_up_kernel(x_ref, g_ref, wg_ref, wu_ref, o_ref, h_ref):
    @pl.when(pl.program_id(1) == 0)
    def _():
        h_ref[...] = _rms(x_ref[...], g_ref[...]).astype(BF16)

    h = h_ref[...]
    gate = jnp.dot(h, wg_ref[...], preferred_element_type=F32)
    up = jnp.dot(h, wu_ref[...], preferred_element_type=F32)
    o_ref[...] = (gate * jax.nn.sigmoid(gate) * up).astype(o_ref.dtype)


def _gate_up(x, g, wg, wu, *, tm, tn):
    n, d = x.shape
    f = wg.shape[1]
    return pl.pallas_call(
        _gate_up_kernel,
        grid=(n // tm, f // tn),
        in_specs=[
            pl.BlockSpec((tm, d), lambda i, j: (i, 0)),
            pl.BlockSpec((1, d), lambda i, j: (0, 0)),
            pl.BlockSpec((d, tn), lambda i, j: (0, j)),
            pl.BlockSpec((d, tn), lambda i, j: (0, j)),
        ],
        out_specs=pl.BlockSpec((tm, tn), lambda i, j: (i, j)),
        out_shape=jax.ShapeDtypeStruct((n, f), BF16),
        scratch_shapes=[pltpu.VMEM((tm, d), BF16)],
        compiler_params=_cparams(("arbitrary", "arbitrary")),
        name="ffn_gate_up",
    )(x, g.reshape(1, -1), wg, wu)


def _down_kernel(a_ref, w_ref, x_ref, o_ref):
    o_ref[...] = x_ref[...] + jnp.dot(a_ref[...], w_ref[...], preferred_element_type=F32)


def _down(act, w, x, *, tm, tn):
    n, f = act.shape
    d = w.shape[1]
    return pl.pallas_call(
        _down_kernel,
        grid=(n // tm, d // tn),
        in_specs=[
            pl.BlockSpec((tm, f), lambda i, j: (i, 0)),
            pl.BlockSpec((f, tn), lambda i, j: (0, j)),
            pl.BlockSpec((tm, tn), lambda i, j: (i, j)),
        ],
        out_specs=pl.BlockSpec((tm, tn), lambda i, j: (i, j)),
        out_shape=jax.ShapeDtypeStruct((n, d), F32),
        compiler_params=_cparams(("arbitrary", "arbitrary")),
        name="ffn_down",
    )(act, w, x)


def _router_kernel(x_ref, g_ref, w_ref, idx_ref, wt_ref):
    h = _rms(x_ref[...], g_ref[...])
    logits = jnp.dot(h, w_ref[...], preferred_element_type=F32,
                     precision=lax.Precision.HIGHEST)
    lane_i = lax.broadcasted_iota(jnp.int32, logits.shape, 1)
    lane = lane_i.astype(F32)
    logits = jnp.where(lane_i < N_EXPERTS, logits, -jnp.inf)
    v1 = jnp.max(logits, axis=-1, keepdims=True)
    i1 = jnp.min(jnp.where(logits == v1, lane, float(LANES)), axis=-1, keepdims=True)
    rest = jnp.where(lane == i1, -jnp.inf, logits)
    v2 = jnp.max(rest, axis=-1, keepdims=True)
    i2 = jnp.min(jnp.where(rest == v2, lane, float(LANES)), axis=-1, keepdims=True)
    e2 = jnp.exp(v2 - v1)
    w1 = 1.0 / (1.0 + e2)
    w2 = e2 / (1.0 + e2)
    idx_ref[...] = jnp.where(lane_i == 0, i1, jnp.where(lane_i == 1, i2, 0.0)).astype(jnp.int32)
    wt_ref[...] = jnp.where(lane_i == 0, w1, jnp.where(lane_i == 1, w2, 0.0))


def _router(x, g, w_pad, *, tm):
    n, d = x.shape
    return pl.pallas_call(
        _router_kernel,
        grid=(n // tm,),
        in_specs=[
            pl.BlockSpec((tm, d), lambda i: (i, 0)),
            pl.BlockSpec((1, d), lambda i: (0, 0)),
            pl.BlockSpec((d, LANES), lambda i: (0, 0)),
        ],
        out_specs=[pl.BlockSpec((tm, LANES), lambda i: (i, 0))] * 2,
        out_shape=[jax.ShapeDtypeStruct((n, LANES), jnp.int32),
                   jax.ShapeDtypeStruct((n, LANES), F32)],
        compiler_params=_cparams(("arbitrary",)),
        name="moe_router",
    )(x, g.reshape(1, -1), w_pad)


def _row_copy(src_hbm, row, dst_vmem, slot, sem):
    return pltpu.make_async_copy(src_hbm.at[pl.ds(row, 1)], dst_vmem.at[pl.ds(slot, 1)], sem)


DMA_UNROLL = 8


def _load_indices(idx_hbm, tile, idx_smem, isem):
    cp = pltpu.make_async_copy(idx_hbm.at[tile], idx_smem, isem)
    cp.start()
    cp.wait()


def _start_rows(idx_smem, src_hbm, dst_vmem, sem):
    def start(i, c):
        _row_copy(src_hbm, idx_smem[i], dst_vmem, i, sem).start()
        return c

    lax.fori_loop(0, dst_vmem.shape[0], start, 0, unroll=DMA_UNROLL)


def _wait_rows(src_hbm, dst_vmem, sem):
    def wait(i, c):
        _row_copy(src_hbm, 0, dst_vmem, i, sem).wait()
        return c

    lax.fori_loop(0, dst_vmem.shape[0], wait, 0, unroll=DMA_UNROLL)


def _gather_norm_kernel(idx_hbm, x_hbm, g_ref, o_ref, idx_smem, buf, isem, sem):
    i = pl.program_id(0)
    n = pl.num_programs(0)

    def fetch(tile, slot):
        _load_indices(idx_hbm, tile, idx_smem, isem)
        _start_rows(idx_smem, x_hbm, buf.at[slot], sem.at[slot])

    @pl.when(i == 0)
    def _():
        fetch(0, 0)

    @pl.when(i + 1 < n)
    def _():
        fetch(i + 1, (i + 1) % 2)

    slot = i % 2
    _wait_rows(x_hbm, buf.at[slot], sem.at[slot])
    o_ref[...] = _rms(buf[slot], g_ref[...]).astype(o_ref.dtype)


def _gather_norm(row_token, x, g, *, tr):
    nt = row_token.shape[0]
    d = x.shape[1]
    return pl.pallas_call(
        _gather_norm_kernel,
        grid=(nt,),
        in_specs=[
            pl.BlockSpec(memory_space=pl.ANY),
            pl.BlockSpec(memory_space=pl.ANY),
            pl.BlockSpec((1, d), lambda i: (0, 0)),
        ],
        out_specs=pl.BlockSpec((tr, d), lambda i: (i, 0)),
        out_shape=jax.ShapeDtypeStruct((nt * tr, d), BF16),
        scratch_shapes=[
            pltpu.SMEM((tr,), jnp.int32),
            pltpu.VMEM((2, tr, d), F32),
            pltpu.SemaphoreType.DMA,
            pltpu.SemaphoreType.DMA((2,)),
        ],
        compiler_params=_cparams(("arbitrary",)),
        name="moe_gather",
    )(row_token, x, g.reshape(1, -1))


def _moe_gate_up_kernel(te_ref, nu_ref, x_ref, wg_ref, wu_ref, o_ref):
    i = pl.program_id(1)

    @pl.when(i < nu_ref[0])
    def _():
        h = x_ref[...]
        gate = jnp.dot(h, wg_ref[0], preferred_element_type=F32)
        up = jnp.dot(h, wu_ref[0], preferred_element_type=F32)
        o_ref[...] = (gate * jax.nn.sigmoid(gate) * up).astype(o_ref.dtype)

    @pl.when(i >= nu_ref[0])
    def _():
        o_ref[...] = jnp.zeros(o_ref.shape, o_ref.dtype)


def _moe_gate_up(tile_expert, n_used, xs, wg, wu, *, tm, tn):
    r, d = xs.shape
    f = wg.shape[2]
    grid_spec = pltpu.PrefetchScalarGridSpec(
        num_scalar_prefetch=2,
        grid=(f // tn, r // tm),
        in_specs=[
            pl.BlockSpec((tm, d), lambda j, i, te, nu: (i, 0)),
            pl.BlockSpec((1, d, tn), lambda j, i, te, nu: (te[i], 0, j)),
            pl.BlockSpec((1, d, tn), lambda j, i, te, nu: (te[i], 0, j)),
        ],
        out_specs=pl.BlockSpec((tm, tn), lambda j, i, te, nu: (i, j)),
    )
    return pl.pallas_call(
        _moe_gate_up_kernel,
        grid_spec=grid_spec,
        out_shape=jax.ShapeDtypeStruct((r, f), BF16),
        compiler_params=_cparams(("arbitrary", "arbitrary")),
        name="moe_gate_up",
    )(tile_expert, n_used, xs, wg, wu)


def _moe_down_kernel(te_ref, nu_ref, a_ref, w_ref, o_ref):
    i = pl.program_id(1)

    @pl.when(i < nu_ref[0])
    def _():
        o_ref[...] = jnp.dot(a_ref[...], w_ref[0], preferred_element_type=F32)

    @pl.when(i >= nu_ref[0])
    def _():
        o_ref[...] = jnp.zeros(o_ref.shape, o_ref.dtype)


def _moe_down(tile_expert, n_used, act, w, *, tm, tn):
    r, f = act.shape
    d = w.shape[2]
    grid_spec = pltpu.PrefetchScalarGridSpec(
        num_scalar_prefetch=2,
        grid=(d // tn, r // tm),
        in_specs=[
            pl.BlockSpec((tm, f), lambda j, i, te, nu: (i, 0)),
            pl.BlockSpec((1, f, tn), lambda j, i, te, nu: (te[i], 0, j)),
        ],
        out_specs=pl.BlockSpec((tm, tn), lambda j, i, te, nu: (i, j)),
    )
    return pl.pallas_call(
        _moe_down_kernel,
        grid_spec=grid_spec,
        out_shape=jax.ShapeDtypeStruct((r, d), F32),
        compiler_params=_cparams(("arbitrary", "arbitrary")),
        name="moe_down",
    )(tile_expert, n_used, act, w)


def _combine_kernel(p0_hbm, p1_hbm, ys_hbm, x_ref, wt_ref, g_ref, o_ref, i0_smem, i1_smem,
                    b0, b1, isem, sem0, sem1, *, final_norm):
    i = pl.program_id(0)
    n = pl.num_programs(0)

    def fetch(tile, slot):
        _load_indices(p0_hbm, tile, i0_smem, isem)
        _load_indices(p1_hbm, tile, i1_smem, isem)
        _start_rows(i0_smem, ys_hbm, b0.at[slot], sem0.at[slot])
        _start_rows(i1_smem, ys_hbm, b1.at[slot], sem1.at[slot])

    @pl.when(i == 0)
    def _():
        fetch(0, 0)

    @pl.when(i + 1 < n)
    def _():
        fetch(i + 1, (i + 1) % 2)

    slot = i % 2
    _wait_rows(ys_hbm, b0.at[slot], sem0.at[slot])
    _wait_rows(ys_hbm, b1.at[slot], sem1.at[slot])
    wt = wt_ref[...]
    y = x_ref[...] + (wt[:, 0:1] * b0[slot] + wt[:, 1:2] * b1[slot])
    if final_norm:
        y = _rms(y, g_ref[...])
    o_ref[...] = y


def _combine(pos0, pos1, ys, x, wt, g, *, tr, final_norm):
    n, d = x.shape
    return pl.pallas_call(
        functools.partial(_combine_kernel, final_norm=final_norm),
        grid=(n // tr,),
        in_specs=[
            pl.BlockSpec(memory_space=pl.ANY),
            pl.BlockSpec(memory_space=pl.ANY),
            pl.BlockSpec(memory_space=pl.ANY),
            pl.BlockSpec((tr, d), lambda i: (i, 0)),
            pl.BlockSpec((tr, LANES), lambda i: (i, 0)),
            pl.BlockSpec((1, d), lambda i: (0, 0)),
        ],
        out_specs=pl.BlockSpec((tr, d), lambda i: (i, 0)),
        out_shape=jax.ShapeDtypeStruct((n, d), F32),
        scratch_shapes=[
            pltpu.SMEM((tr,), jnp.int32),
            pltpu.SMEM((tr,), jnp.int32),
            pltpu.VMEM((2, tr, d), F32),
            pltpu.VMEM((2, tr, d), F32),
            pltpu.SemaphoreType.DMA,
            pltpu.SemaphoreType.DMA((2,)),
            pltpu.SemaphoreType.DMA((2,)),
        ],
        compiler_params=_cparams(("arbitrary",)),
        name="moe_combine",
    )(pos0.reshape(n // tr, tr), pos1.reshape(n // tr, tr), ys, x, wt, g.reshape(1, -1))


def _rmsnorm_kernel(x_ref, g_ref, o_ref):
    o_ref[...] = _rms(x_ref[...], g_ref[...])


def _final_norm(x, g, *, tm):
    n, d = x.shape
    return pl.pallas_call(
        _rmsnorm_kernel,
        grid=(n // tm,),
        in_specs=[pl.BlockSpec((tm, d), lambda i: (i, 0)),
                  pl.BlockSpec((1, d), lambda i: (0, 0))],
        out_specs=pl.BlockSpec((tm, d), lambda i: (i, 0)),
        out_shape=jax.ShapeDtypeStruct((n, d), F32),
        compiler_params=_cparams(("arbitrary",)),
        name="final_norm",
    )(x, g.reshape(1, -1))


def _routing_plan(top_idx, *, tm):
    n = top_idx.shape[0]
    n_assign = 2 * n
    rows = n_assign + N_EXPERTS * tm
    n_tiles = rows // tm
    e_flat = top_idx.reshape(n_assign)
    onehot = (e_flat[:, None] == jnp.arange(N_EXPERTS, dtype=jnp.int32)[None, :]).astype(jnp.int32)
    counts = jnp.sum(onehot, axis=0)
    rank = jnp.sum((jnp.cumsum(onehot, axis=0) - onehot) * onehot, axis=1)
    padded = ((counts + tm - 1) // tm) * tm
    ends = jnp.cumsum(padded)
    starts = ends - padded
    dest = starts[e_flat] + rank
    token = jnp.arange(n_assign, dtype=jnp.int32) // 2
    row_token = jnp.zeros((rows,), jnp.int32).at[dest].set(token)
    tile_start = jnp.arange(n_tiles, dtype=jnp.int32) * tm
    tile_expert = jnp.sum((tile_start[:, None] >= ends[None, :]).astype(jnp.int32), axis=1)
    tile_expert = jnp.minimum(tile_expert, N_EXPERTS - 1).astype(jnp.int32)
    n_used = (ends[-1] // tm).astype(jnp.int32).reshape(1)
    pos = dest.reshape(n, 2).astype(jnp.int32)
    return row_token.reshape(n_tiles, tm), tile_expert, n_used, pos


def _moe_layer(x, g, router_pad, wg, wu, wd, final_g, *, final_norm):
    tm = 512
    idx, wt = _router(x, g, router_pad, tm=1024)
    row_token, tile_expert, n_used, pos = _routing_plan(idx[:, :2], tm=tm)
    xs = _gather_norm(row_token, x, g, tr=tm)
    act = _moe_gate_up(tile_expert, n_used, xs, wg, wu, tm=tm, tn=1024)
    ys = _moe_down(tile_expert, n_used, act, wd, tm=tm, tn=512)
    return _combine(pos[:, 0], pos[:, 1], ys, x, wt, final_g, tr=256, final_norm=final_norm)


def _rope_tables(seq):
    pos = jnp.arange(seq, dtype=F32)[:, None]

    def tab(dim):
        inv = ROPE_THETA ** (-jnp.arange(0, dim, 2, dtype=F32) / dim)
        ang = pos * inv[None, :]
        return jnp.cos(ang), jnp.sin(ang)

    cf, sf = tab(HEAD_DIM)
    cos_f = jnp.concatenate([cf, cf], axis=1)
    sin_f = jnp.concatenate([-sf, sf], axis=1)
    cr, sr = tab(ROPE_DIM)
    zero = jnp.zeros((seq, LANES - ROPE_DIM), F32)
    cos_r = jnp.concatenate([cr, cr, zero], axis=1)
    sin_r = jnp.concatenate([-sr, sr, zero], axis=1)
    return cos_f, sin_f, cos_r, sin_r


def kernel(x, attn_norm_g, w_in, q_norm_g, kv_norm_g, w_uq, w_uk, w_uv, out_a_norm_g,
           out_b_norm_g, w_out, ffn_norm_g, dense_w_gate, dense_w_up, dense_w_down,
           router_w, moe_w_gate, moe_w_up, moe_w_down, final_norm_g):
    batch, seq, d_model = x.shape
    depth = w_in.shape[0]
    n = batch * seq
    mix = DIL_HEADS * HEAD_DIM
    cos_f, sin_f, cos_r, sin_r = _rope_tables(seq)
    dil_bias = _dil_bias(DIL_TQ)

    c0, c1, c2 = Q_LORA, Q_LORA + KV_LORA, Q_LORA + KV_LORA + ROPE_DIM
    pad = jnp.zeros((depth, d_model, 1024 - Q_LORA - ROPE_DIM), F32)
    w_proj = jnp.concatenate([w_in[:, :, :c0], w_in[:, :, c1:c2], pad, w_in[:, :, c0:c1],
                              w_in[:, :, c2:]], axis=2).astype(BF16)
    lat_tiles = LATENT_COLS // PROJ_TN
    uq = w_uq.reshape(depth, Q_LORA, MLA_HEADS, QK_DIM)
    uq = jnp.concatenate([uq, jnp.zeros((depth, Q_LORA, MLA_HEADS, 2 * LANES - QK_DIM), F32)],
                         axis=3)
    w_uq_p = uq.reshape(depth, Q_LORA, MLA_HEADS * 2 * LANES).astype(BF16)
    w_ukv = jnp.concatenate([w_uk, w_uv], axis=2).astype(BF16)
    w_out_b = w_out.astype(BF16)
    dense_g = dense_w_gate.astype(BF16)
    dense_u = dense_w_up.astype(BF16)
    dense_d = dense_w_down.astype(BF16)
    moe_g = moe_w_gate.astype(BF16)
    moe_u = moe_w_up.astype(BF16)
    moe_d = moe_w_down.astype(BF16)
    router_pad = jnp.concatenate(
        [router_w, jnp.zeros(router_w.shape[:2] + (LANES - N_EXPERTS,), F32)], axis=2)

    xf = x.reshape(n, d_model)
    for layer in range(depth):
        g_attn = attn_norm_g[layer]
        proj = _norm_mm(xf, g_attn, w_proj[layer], cos_f, sin_f, x_block=(d_model, 0),
                        k_norm=d_model, tm=1024, tn=PROJ_TN,
                        rope_tiles=(lat_tiles, lat_tiles + 2 * mix // PROJ_TN),
                        scale_tiles=lat_tiles + mix // PROJ_TN,
                        scale=LOG2_E * HEAD_DIM ** -0.5, name="in_proj")
        q, kr = _q_up(proj, q_norm_g[layer], w_uq_p[layer], cos_r, sin_r, tm=1024, tn=512,
                      scale=LOG2_E * QK_DIM ** -0.5)
        kv = _norm_mm(proj, kv_norm_g[layer], w_ukv[layer], cos_f, sin_f,
                      x_block=(KV_LORA, 1024 // KV_LORA), k_norm=KV_LORA, tm=1024, tn=1024,
                      name="kv_up")
        o_a = _mla_attention(q, kv, kr, batch=batch, seq=seq, tq=1024, tk=512)
        o_b = _dilated(proj, dil_bias, batch=batch, seq=seq, tq=DIL_TQ, ck=256)
        xf = _out_proj(o_a, o_b, out_a_norm_g[layer], out_b_norm_g[layer],
                       w_out_b[layer], xf, tm=1024, tn=512)
        j = layer // 2
        last = layer == depth - 1
        if layer % 2 == 0:
            act = _gate_up(xf, ffn_norm_g[layer], dense_g[j], dense_u[j], tm=1024, tn=512)
            xf = _down(act, dense_d[j], xf, tm=1024, tn=512)
            if last:
                xf = _final_norm(xf, final_norm_g, tm=1024)
        else:
            xf = _moe_layer(xf, ffn_norm_g[layer], router_pad[j], moe_g[j], moe_u[j], moe_d[j],
                            final_norm_g, final_norm=last)
    return xf.reshape(batch, seq, d_model)
```

```python
import functools

import jax
import jax.numpy as jnp
from jax import lax
from jax.experimental import pallas as pl
from jax.experimental.pallas import tpu as pltpu

F32 = jnp.float32
BF16 = jnp.bfloat16

LANES = 128
BF16_SUBLANES = 16
MLA_HEADS = 8
DIL_HEADS = 8
HEAD_DIM = 128
Q_LORA = 768
KV_LORA = 512
ROPE_DIM = 64
QK_DIM = HEAD_DIM + ROPE_DIM
N_EXPERTS = 8
DIL_CONFIGS = ((128, 1), (512, 4), (2048, 16))
DIL_PAD = max(w for w, _ in DIL_CONFIGS)
DIL_TQ = 512
LATENT_COLS = 1536
PROJ_TN = 512
ROPE_THETA = 10000.0
EPS = 1e-6
NEG_INF = -1e30
LOG2_E = 1.4426950408889634
VMEM_LIMIT = 56 * 1024 * 1024


def _cparams(sem):
    return pltpu.CompilerParams(dimension_semantics=sem, vmem_limit_bytes=VMEM_LIMIT)


def _rms(xf, g):
    ms = jnp.mean(xf * xf, axis=-1, keepdims=True)
    return xf * lax.rsqrt(ms + EPS) * g


def _rope_full(a, c, s):
    return a * c + pltpu.roll(a, HEAD_DIM // 2, 1) * s


def _rope_half(a, c, s):
    lane = lax.broadcasted_iota(jnp.int32, a.shape, 1)
    lower = (lane % ROPE_DIM) < (ROPE_DIM // 2)
    partner = jnp.where(lower, pltpu.roll(a, LANES - ROPE_DIM // 2, 1),
                        pltpu.roll(a, ROPE_DIM // 2, 1))
    return a * c + partner * s


def _norm_mm_kernel(x_ref, g_ref, w_ref, cos_ref, sin_ref, o_ref, h_ref, *,
                    k_norm, rope_tiles, scale_tiles, scale):
    j = pl.program_id(1)

    @pl.when(j == 0)
    def _():
        xf = x_ref[:, :k_norm].astype(F32)
        h_ref[...] = _rms(xf, g_ref[...]).astype(BF16)

    acc = jnp.dot(h_ref[...], w_ref[...], preferred_element_type=F32)
    if rope_tiles is None:
        o_ref[...] = acc.astype(o_ref.dtype)
        return
    rope = (j >= rope_tiles[0]) & (j < rope_tiles[1])

    @pl.when(rope)
    def _():
        scaled = j < scale_tiles
        c = jnp.where(scaled, cos_ref[...] * scale, cos_ref[...])
        s = jnp.where(scaled, sin_ref[...] * scale, sin_ref[...])
        for k in range(acc.shape[1] // HEAD_DIM):
            sl = slice(k * HEAD_DIM, (k + 1) * HEAD_DIM)
            o_ref[:, sl] = _rope_full(acc[:, sl], c, s).astype(o_ref.dtype)

    @pl.when(jnp.logical_not(rope))
    def _():
        o_ref[...] = acc.astype(o_ref.dtype)


def _norm_mm(x, g, w, cos, sin, *, x_block, k_norm, tm, tn, rope_tiles=None, scale_tiles=0,
             scale=1.0, name):
    n = x.shape[0]
    xw, xb = x_block
    kw, nout = w.shape
    assert kw == k_norm and n % tm == 0 and nout % tn == 0
    ns = cos.shape[0] // tm
    kern = functools.partial(_norm_mm_kernel, k_norm=k_norm, rope_tiles=rope_tiles,
                             scale_tiles=scale_tiles, scale=scale)
    return pl.pallas_call(
        kern,
        grid=(n // tm, nout // tn),
        in_specs=[
            pl.BlockSpec((tm, xw), lambda i, j: (i, xb)),
            pl.BlockSpec((1, k_norm), lambda i, j: (0, 0)),
            pl.BlockSpec((k_norm, tn), lambda i, j: (0, j)),
            pl.BlockSpec((tm, HEAD_DIM), lambda i, j: (i % ns, 0)),
            pl.BlockSpec((tm, HEAD_DIM), lambda i, j: (i % ns, 0)),
        ],
        out_specs=pl.BlockSpec((tm, tn), lambda i, j: (i, j)),
        out_shape=jax.ShapeDtypeStruct((n, nout), BF16),
        scratch_shapes=[pltpu.VMEM((tm, k_norm), BF16)],
        compiler_params=_cparams(("arbitrary", "arbitrary")),
        name=name,
    )(x, g.reshape(1, -1), w, cos, sin)


def _q_up_kernel(x_ref, g_ref, w_ref, cos_ref, sin_ref, q_ref, kr_ref, h_ref, *, scale):
    j = pl.program_id(1)
    c = cos_ref[...]
    s = sin_ref[...]

    @pl.when(j == 0)
    def _():
        blk = x_ref[...].astype(F32)
        h_ref[...] = _rms(blk[:, :Q_LORA], g_ref[...]).astype(BF16)
        kr_ref[...] = _rope_half(blk[:, Q_LORA:Q_LORA + LANES], c, s).astype(BF16)

    acc = jnp.dot(h_ref[...], w_ref[...], preferred_element_type=F32)
    for hh in range(acc.shape[1] // (2 * LANES)):
        lo = hh * 2 * LANES
        q_ref[:, lo:lo + LANES] = (acc[:, lo:lo + LANES] * scale).astype(BF16)
        q_ref[:, lo + LANES:lo + 2 * LANES] = (
            _rope_half(acc[:, lo + LANES:lo + 2 * LANES], c, s) * scale).astype(BF16)


def _q_up(lat, g, w, cos, sin, *, tm, tn, scale):
    n = lat.shape[0]
    nout = w.shape[1]
    ns = cos.shape[0] // tm
    return pl.pallas_call(
        functools.partial(_q_up_kernel, scale=scale),
        grid=(n // tm, nout // tn),
        in_specs=[
            pl.BlockSpec((tm, 1024), lambda i, j: (i, 0)),
            pl.BlockSpec((1, Q_LORA), lambda i, j: (0, 0)),
            pl.BlockSpec((Q_LORA, tn), lambda i, j: (0, j)),
            pl.BlockSpec((tm, LANES), lambda i, j: (i % ns, 0)),
            pl.BlockSpec((tm, LANES), lambda i, j: (i % ns, 0)),
        ],
        out_specs=[
            pl.BlockSpec((tm, tn), lambda i, j: (i, j)),
            pl.BlockSpec((tm, LANES), lambda i, j: (i, 0)),
        ],
        out_shape=[jax.ShapeDtypeStruct((n, nout), BF16),
                   jax.ShapeDtypeStruct((n, LANES), BF16)],
        scratch_shapes=[pltpu.VMEM((tm, Q_LORA), BF16)],
        compiler_params=_cparams(("arbitrary", "arbitrary")),
        name="q_up",
    )(lat, g.reshape(1, -1), w, cos, sin)


def _mla_kernel(q_ref, kn_ref, kr_ref, v_ref, o_ref, kcat_ref, vt_ref, qt_ref, m_ref,
                acc_ref, s_ref, *, tq, tk):
    qi = pl.program_id(2)
    seq = kcat_ref.shape[0]
    vt_rows = vt_ref.shape[1]

    @pl.when(qi == 0)
    def _():
        kcat_ref[:, :HEAD_DIM] = kn_ref[...]
        kcat_ref[:, HEAD_DIM:] = kr_ref[...]
        ones_row = (lax.broadcasted_iota(jnp.int32, (vt_rows - HEAD_DIM, tk), 0) == 0)

        def transpose_chunk(c, carry):
            off = pl.multiple_of(c * tk, tk)
            vt_ref[c, :HEAD_DIM, :] = v_ref[pl.ds(off, tk), :].astype(F32).T.astype(BF16)
            vt_ref[c, HEAD_DIM:, :] = ones_row.astype(BF16)
            return carry

        lax.fori_loop(0, seq // tk, transpose_chunk, 0)

    qt_ref[...] = q_ref[...].astype(F32).T.astype(BF16)
    m_ref[...] = jnp.full(m_ref.shape, NEG_INF, F32)
    acc_ref[...] = jnp.zeros(acc_ref.shape, F32)
    per_q = tq // tk

    def scores(j):
        off = pl.multiple_of(j * tk, tk)
        return jnp.dot(kcat_ref[pl.ds(off, tk), :], qt_ref[...],
                       preferred_element_type=F32)

    def chunk(j, masked, has_next):
        s = s_ref[j % 2]
        if has_next:
            s_ref[(j + 1) % 2] = scores(j + 1)
        if masked:
            key = j * tk + lax.broadcasted_iota(jnp.int32, (tk, tq), 0)
            qry = qi * tq + lax.broadcasted_iota(jnp.int32, (tk, tq), 1)
            s = jnp.where(key <= qry, s, NEG_INF)
        m_old = m_ref[...]
        m_new = jnp.maximum(m_old, jnp.max(s, axis=0, keepdims=True))
        a = jnp.exp2(m_old - m_new)
        p = jnp.exp2(s - m_new).astype(BF16)
        acc_ref[...] = a * acc_ref[...] + jnp.dot(vt_ref[j], p, preferred_element_type=F32)
        m_ref[...] = m_new

    def body(j, carry):
        chunk(j, False, True)
        return carry

    n_full = qi * per_q
    s_ref[0] = scores(0)
    lax.fori_loop(0, n_full, body, 0)
    for d in range(per_q):
        chunk(n_full + d, True, d + 1 < per_q)
    out_t = acc_ref[:HEAD_DIM, :] / acc_ref[HEAD_DIM:HEAD_DIM + 1, :]
    o_ref[...] = out_t.T.astype(o_ref.dtype)


def _mla_attention(q, kv, kr, *, batch, seq, tq, tk):
    n = q.shape[0]
    nq = seq // tq
    return pl.pallas_call(
        functools.partial(_mla_kernel, tq=tq, tk=tk),
        grid=(batch, MLA_HEADS, nq),
        in_specs=[
            pl.BlockSpec((tq, 2 * LANES), lambda b, h, i: (b * nq + i, h)),
            pl.BlockSpec((seq, HEAD_DIM), lambda b, h, i: (b, h)),
            pl.BlockSpec((seq, LANES), lambda b, h, i: (b, 0)),
            pl.BlockSpec((seq, HEAD_DIM), lambda b, h, i: (b, MLA_HEADS + h)),
        ],
        out_specs=pl.BlockSpec((tq, HEAD_DIM), lambda b, h, i: (b * nq + i, h)),
        out_shape=jax.ShapeDtypeStruct((n, MLA_HEADS * HEAD_DIM), BF16),
        scratch_shapes=[
            pltpu.VMEM((seq, 2 * LANES), BF16),
            pltpu.VMEM((seq // tk, HEAD_DIM + BF16_SUBLANES, tk), BF16),
            pltpu.VMEM((2 * LANES, tq), BF16),
            pltpu.VMEM((1, tq), F32),
            pltpu.VMEM((HEAD_DIM + BF16_SUBLANES, tq), F32),
            pltpu.VMEM((2, tk, tq), F32),
        ],
        compiler_params=_cparams(("arbitrary", "arbitrary", "arbitrary")),
        name="mla_attention",
    )(q, kv, kr, kv)


def _dil_bias(tq):
    kk = jnp.arange(DIL_PAD + tq, dtype=jnp.int32)[:, None]
    qq = jnp.arange(tq, dtype=jnp.int32)[None, :]
    delta = qq + DIL_PAD - kk
    count = jnp.zeros(delta.shape, F32)
    for win, dil in DIL_CONFIGS:
        count += ((delta >= 0) & (delta <= win) & (delta % dil == 0)).astype(F32)
    return jnp.where(count > 0, jnp.log2(jnp.maximum(count, 1.0)), NEG_INF)


def _dil_kernel(q_ref, k_ref, v_ref, bias_ref, o_ref, kpad_ref, vt_ref, s_ref, *, tq):
    i = pl.program_id(2)
    seq = k_ref.shape[0]
    ck = vt_ref.shape[2]
    vt_rows = vt_ref.shape[1]
    n_pad = DIL_PAD // ck
    win = DIL_PAD + tq

    @pl.when(i == 0)
    def _():
        kpad_ref[:DIL_PAD, :] = jnp.zeros((DIL_PAD, HEAD_DIM), BF16)
        kpad_ref[DIL_PAD:, :] = k_ref[...]
        ones_row = (lax.broadcasted_iota(jnp.int32, (vt_rows - HEAD_DIM, ck), 0) == 0)
        for c in range(n_pad):
            vt_ref[c] = jnp.zeros((vt_rows, ck), BF16)

        def transpose_chunk(c, carry):
            off = pl.multiple_of(c * ck, ck)
            vt_ref[n_pad + c, :HEAD_DIM, :] = v_ref[pl.ds(off, ck), :].astype(F32).T.astype(BF16)
            vt_ref[n_pad + c, HEAD_DIM:, :] = ones_row.astype(BF16)
            return carry

        lax.fori_loop(0, seq // ck, transpose_chunk, 0)

    s0 = pl.multiple_of(i * tq, tq)
    qt = q_ref[...].astype(F32).T.astype(BF16)
    s_ref[...] = jnp.dot(kpad_ref[pl.ds(s0, win), :], qt,
                         preferred_element_type=F32) + bias_ref[...]

    @pl.when(s0 < DIL_PAD)
    def _():
        row = lax.broadcasted_iota(jnp.int32, (win, tq), 0)
        s_ref[...] = jnp.where(row >= DIL_PAD - s0, s_ref[...], NEG_INF)

    s = s_ref[...]
    m = jnp.max(s, axis=0, keepdims=True)
    p = jnp.exp2(s - m).astype(BF16)
    base = i * (tq // ck)
    acc = jnp.zeros((vt_rows, tq), F32)
    for c in range(win // ck):
        acc += jnp.dot(vt_ref[base + c], p[c * ck:(c + 1) * ck, :], preferred_element_type=F32)
    o_ref[...] = (acc[:HEAD_DIM, :] / acc[HEAD_DIM:HEAD_DIM + 1, :]).T


def _dilated(qkv, bias, *, batch, seq, tq, ck):
    n = qkv.shape[0]
    nq = seq // tq
    win = DIL_PAD + tq
    vt_rows = HEAD_DIM + BF16_SUBLANES
    q0 = LATENT_COLS // HEAD_DIM
    return pl.pallas_call(
        functools.partial(_dil_kernel, tq=tq),
        grid=(batch, DIL_HEADS, nq),
        in_specs=[
            pl.BlockSpec((tq, HEAD_DIM), lambda b, h, i: (b * nq + i, q0 + h)),
            pl.BlockSpec((seq, HEAD_DIM), lambda b, h, i: (b, q0 + DIL_HEADS + h)),
            pl.BlockSpec((seq, HEAD_DIM), lambda b, h, i: (b, q0 + 2 * DIL_HEADS + h)),
            pl.BlockSpec((win, tq), lambda b, h, i: (0, 0)),
        ],
        out_specs=pl.BlockSpec((tq, HEAD_DIM), lambda b, h, i: (b * nq + i, h)),
        out_shape=jax.ShapeDtypeStruct((n, DIL_HEADS * HEAD_DIM), F32),
        scratch_shapes=[
            pltpu.VMEM((DIL_PAD + seq, HEAD_DIM), BF16),
            pltpu.VMEM(((DIL_PAD + seq) // ck, vt_rows, ck), BF16),
            pltpu.VMEM((win, tq), F32),
        ],
        compiler_params=_cparams(("arbitrary", "arbitrary", "arbitrary")),
        name="dilated_attention",
    )(qkv, qkv, qkv, bias)


def _out_proj_kernel(oa_ref, ob_ref, ga_ref, gb_ref, w_ref, x_ref, o_ref, h_ref):
    half = oa_ref.shape[1]

    @pl.when(pl.program_id(1) == 0)
    def _():
        h_ref[:, :half] = _rms(oa_ref[...].astype(F32), ga_ref[...]).astype(BF16)
        h_ref[:, half:] = _rms(ob_ref[...], gb_ref[...]).astype(BF16)

    o_ref[...] = x_ref[...] + jnp.dot(h_ref[...], w_ref[...], preferred_element_type=F32)


def _out_proj(o_a, o_b, g_a, g_b, w, x, *, tm, tn):
    n, d = x.shape
    half = o_a.shape[1]
    row = lambda i, j: (i, 0)
    return pl.pallas_call(
        _out_proj_kernel,
        grid=(n // tm, d // tn),
        in_specs=[pl.BlockSpec((tm, half), row)] * 2
        + [pl.BlockSpec((1, half), lambda i, j: (0, 0))] * 2
        + [pl.BlockSpec((2 * half, tn), lambda i, j: (0, j)),
           pl.BlockSpec((tm, tn), lambda i, j: (i, j))],
        out_specs=pl.BlockSpec((tm, tn), lambda i, j: (i, j)),
        out_shape=jax.ShapeDtypeStruct((n, d), F32),
        scratch_shapes=[pltpu.VMEM((tm, 2 * half), BF16)],
        compiler_params=_cparams(("arbitrary", "arbitrary")),
        name="out_proj",
    )(o_a, o_b, g_a.reshape(1, -1), g_b.reshape(1, -1), w, x)


def _gate_up_kernel(x_ref, g_ref, wg_ref, wu_ref, o_ref, h_ref):
    @pl.when(pl.program_id(1) == 0)
    def _():
        h_ref[...] = _rms(x_ref[...], g_ref[...]).astype(BF16)

    h = h_ref[...]
    gate = jnp.dot(h, wg_ref[...], preferred_element_type=F32)
    up = jnp.dot(h, wu_ref[...], preferred_element_type=F32)
    o_ref[...] = (gate * jax.nn.sigmoid(gate) * up).astype(o_ref.dtype)


def _gate_up(x, g, wg, wu, *, tm, tn):
    n, d = x.shape
    f = wg.shape[1]
    return pl.pallas_call(
        _gate_up_kernel,
        grid=(n // tm, f // tn),
        in_specs=[
            pl.BlockSpec((tm, d), lambda i, j: (i, 0)),
            pl.BlockSpec((1, d), lambda i, j: (0, 0)),
            pl.BlockSpec((d, tn), lambda i, j: (0, j)),
            pl.BlockSpec((d, tn), lambda i, j: (0, j)),
        ],
        out_specs=pl.BlockSpec((tm, tn), lambda i, j: (i, j)),
        out_shape=jax.ShapeDtypeStruct((n, f), BF16),
        scratch_shapes=[pltpu.VMEM((tm, d), BF16)],
        compiler_params=_cparams(("arbitrary", "arbitrary")),
        name="ffn_gate_up",
    )(x, g.reshape(1, -1), wg, wu)


def _down_kernel(a_ref, w_ref, x_ref, o_ref):
    o_ref[...] = x_ref[...] + jnp.dot(a_ref[...], w_ref[...], preferred_element_type=F32)


def _down(act, w, x, *, tm, tn):
    n, f = act.shape
    d = w.shape[1]
    return pl.pallas_call(
        _down_kernel,
        grid=(n // tm, d // tn),
        in_specs=[
            pl.BlockSpec((tm, f), lambda i, j: (i, 0)),
            pl.BlockSpec((f, tn), lambda i, j: (0, j)),
            pl.BlockSpec((tm, tn), lambda i, j: (i, j)),
        ],
        out_specs=pl.BlockSpec((tm, tn), lambda i, j: (i, j)),
        out_shape=jax.ShapeDtypeStruct((n, d), F32),
        compiler_params=_cparams(("arbitrary", "arbitrary")),
        name="ffn_down",
    )(act, w, x)


def _router_kernel(x_ref, g_ref, w_ref, idx_ref, wt_ref):
    h = _rms(x_ref[...], g_ref[...])
    logits = jnp.dot(h, w_ref[...], preferred_element_type=F32,
                     precision=lax.Precision.HIGHEST)
    lane_i = lax.broadcasted_iota(jnp.int32, logits.shape, 1)
    lane = lane_i.astype(F32)
    logits = jnp.where(lane_i < N_EXPERTS, logits, -jnp.inf)
    v1 = jnp.max(logits, axis=-1, keepdims=True)
    i1 = jnp.min(jnp.where(logits == v1, lane, float(LANES)), axis=-1, keepdims=True)
    rest = jnp.where(lane == i1, -jnp.inf, logits)
    v2 = jnp.max(rest, axis=-1, keepdims=True)
    i2 = jnp.min(jnp.where(rest == v2, lane, float(LANES)), axis=-1, keepdims=True)
    e2 = jnp.exp(v2 - v1)
    w1 = 1.0 / (1.0 + e2)
    w2 = e2 / (1.0 + e2)
    idx_ref[...] = jnp.where(lane_i == 0, i1, jnp.where(lane_i == 1, i2, 0.0)).astype(jnp.int32)
    wt_ref[...] = jnp.where(lane_i == 0, w1, jnp.where(lane_i == 1, w2, 0.0))


def _router(x, g, w_pad, *, tm):
    n, d = x.shape
    return pl.pallas_call(
        _router_kernel,
        grid=(n // tm,),
        in_specs=[
            pl.BlockSpec((tm, d), lambda i: (i, 0)),
            pl.BlockSpec((1, d), lambda i: (0, 0)),
            pl.BlockSpec((d, LANES), lambda i: (0, 0)),
        ],
        out_specs=[pl.BlockSpec((tm, LANES), lambda i: (i, 0))] * 2,
        out_shape=[jax.ShapeDtypeStruct((n, LANES), jnp.int32),
                   jax.ShapeDtypeStruct((n, LANES), F32)],
        compiler_params=_cparams(("arbitrary",)),
        name="moe_router",
    )(x, g.reshape(1, -1), w_pad)


def _row_copy(src_hbm, row, dst_vmem, slot, sem):
    return pltpu.make_async_copy(src_hbm.at[pl.ds(row, 1)], dst_vmem.at[pl.ds(slot, 1)], sem)


DMA_UNROLL = 8
DMA_PRIORITIES = 2


def _load_indices(idx_hbm, tile, idx_smem, isem):
    cp = pltpu.make_async_copy(idx_hbm.at[tile], idx_smem, isem)
    cp.start()
    cp.wait()


def _start_rows(idx_smem, src_hbm, dst_vmem, sem):
    def start(i, c):
        for prio in range(DMA_PRIORITIES):
            row = DMA_PRIORITIES * i + prio
            _row_copy(src_hbm, idx_smem[row], dst_vmem, row, sem).start(priority=prio)
        return c

    lax.fori_loop(0, dst_vmem.shape[0] // DMA_PRIORITIES, start, 0,
                  unroll=DMA_UNROLL // DMA_PRIORITIES)


def _wait_rows(src_hbm, dst_vmem, sem):
    def wait(i, c):
        _row_copy(src_hbm, 0, dst_vmem, i, sem).wait()
        return c

    lax.fori_loop(0, dst_vmem.shape[0], wait, 0, unroll=DMA_UNROLL)


def _gather_norm_kernel(nu_ref, idx_hbm, x_hbm, g_ref, o_ref, idx_smem, buf, isem, sem):
    i = pl.program_id(0)
    n_used = nu_ref[0]

    def fetch(tile, slot):
        _load_indices(idx_hbm, tile, idx_smem, isem)
        _start_rows(idx_smem, x_hbm, buf.at[slot], sem.at[slot])

    @pl.when((i == 0) & (n_used > 0))
    def _():
        fetch(0, 0)

    @pl.when(i + 1 < n_used)
    def _():
        fetch(i + 1, (i + 1) % 2)

    @pl.when(i < n_used)
    def _():
        slot = i % 2
        _wait_rows(x_hbm, buf.at[slot], sem.at[slot])
        o_ref[...] = _rms(buf[slot], g_ref[...]).astype(o_ref.dtype)

    @pl.when(i >= n_used)
    def _():
        o_ref[...] = jnp.zeros(o_ref.shape, o_ref.dtype)


def _gather_norm(n_used, row_token, x, g, *, tr):
    nt = row_token.shape[0]
    d = x.shape[1]
    grid_spec = pltpu.PrefetchScalarGridSpec(
        num_scalar_prefetch=1,
        grid=(nt,),
        in_specs=[
            pl.BlockSpec(memory_space=pl.ANY),
            pl.BlockSpec(memory_space=pl.ANY),
            pl.BlockSpec((1, d), lambda i, nu: (0, 0)),
        ],
        out_specs=pl.BlockSpec((tr, d), lambda i, nu: (i, 0)),
        scratch_shapes=[
            pltpu.SMEM((tr,), jnp.int32),
            pltpu.VMEM((2, tr, d), F32),
            pltpu.SemaphoreType.DMA,
            pltpu.SemaphoreType.DMA((2,)),
        ],
    )
    return pl.pallas_call(
        _gather_norm_kernel,
        grid_spec=grid_spec,
        out_shape=jax.ShapeDtypeStruct((nt * tr, d), BF16),
        compiler_params=_cparams(("arbitrary",)),
        name="moe_gather",
    )(n_used, row_token, x, g.reshape(1, -1))


def _moe_gate_up_kernel(te_ref, nu_ref, x_ref, wg_ref, wu_ref, o_ref):
    i = pl.program_id(1)

    @pl.when(i < nu_ref[0])
    def _():
        h = x_ref[...]
        gate = jnp.dot(h, wg_ref[0], preferred_element_type=F32)
        up = jnp.dot(h, wu_ref[0], preferred_element_type=F32)
        o_ref[...] = (gate * jax.nn.sigmoid(gate) * up).astype(o_ref.dtype)

    @pl.when(i >= nu_ref[0])
    def _():
        o_ref[...] = jnp.zeros(o_ref.shape, o_ref.dtype)


def _moe_gate_up(tile_expert, n_used, xs, wg, wu, *, tm, tn):
    r, d = xs.shape
    f = wg.shape[2]
    grid_spec = pltpu.PrefetchScalarGridSpec(
        num_scalar_prefetch=2,
        grid=(f // tn, r // tm),
        in_specs=[
            pl.BlockSpec((tm, d), lambda j, i, te, nu: (i, 0)),
            pl.BlockSpec((1, d, tn), lambda j, i, te, nu: (te[i], 0, j)),
            pl.BlockSpec((1, d, tn), lambda j, i, te, nu: (te[i], 0, j)),
        ],
        out_specs=pl.BlockSpec((tm, tn), lambda j, i, te, nu: (i, j)),
    )
    return pl.pallas_call(
        _moe_gate_up_kernel,
        grid_spec=grid_spec,
        out_shape=jax.ShapeDtypeStruct((r, f), BF16),
        compiler_params=_cparams(("arbitrary", "arbitrary")),
        name="moe_gate_up",
    )(tile_expert, n_used, xs, wg, wu)


def _moe_down_kernel(te_ref, nu_ref, a_ref, w_ref, o_ref):
    i = pl.program_id(1)

    @pl.when(i < nu_ref[0])
    def _():
        o_ref[...] = jnp.dot(a_ref[...], w_ref[0], preferred_element_type=F32)

    @pl.when(i >= nu_ref[0])
    def _():
        o_ref[...] = jnp.zeros(o_ref.shape, o_ref.dtype)


def _moe_down(tile_expert, n_used, act, w, *, tm, tn):
    r, f = act.shape
    d = w.shape[2]
    grid_spec = pltpu.PrefetchScalarGridSpec(
        num_scalar_prefetch=2,
        grid=(d // tn, r // tm),
        in_specs=[
            pl.BlockSpec((tm, f), lambda j, i, te, nu: (i, 0)),
            pl.BlockSpec((1, f, tn), lambda j, i, te, nu: (te[i], 0, j)),
        ],
        out_specs=pl.BlockSpec((tm, tn), lambda j, i, te, nu: (i, j)),
    )
    return pl.pallas_call(
        _moe_down_kernel,
        grid_spec=grid_spec,
        out_shape=jax.ShapeDtypeStruct((r, d), F32),
        compiler_params=_cparams(("arbitrary", "arbitrary")),
        name="moe_down",
    )(tile_expert, n_used, act, w)


def _combine_kernel(p0_hbm, p1_hbm, ys_hbm, x_ref, wt_ref, g_ref, o_ref, i0_smem, i1_smem,
                    b0, b1, isem, sem0, sem1, *, final_norm):
    i = pl.program_id(0)
    n = pl.num_programs(0)

    def fetch(tile, slot):
        _load_indices(p0_hbm, tile, i0_smem, isem)
        _load_indices(p1_hbm, tile, i1_smem, isem)
        _start_rows(i0_smem, ys_hbm, b0.at[slot], sem0.at[slot])
        _start_rows(i1_smem, ys_hbm, b1.at[slot], sem1.at[slot])

    @pl.when(i == 0)
    def _():
        fetch(0, 0)

    @pl.when(i + 1 < n)
    def _():
        fetch(i + 1, (i + 1) % 2)

    slot = i % 2
    _wait_rows(ys_hbm, b0.at[slot], sem0.at[slot])
    _wait_rows(ys_hbm, b1.at[slot], sem1.at[slot])
    wt = wt_ref[...]
    y = x_ref[...] + (wt[:, 0:1] * b0[slot] + wt[:, 1:2] * b1[slot])
    if final_norm:
        y = _rms(y, g_ref[...])
    o_ref[...] = y


def _combine(pos0, pos1, ys, x, wt, g, *, tr, final_norm):
    n, d = x.shape
    return pl.pallas_call(
        functools.partial(_combine_kernel, final_norm=final_norm),
        grid=(n // tr,),
        in_specs=[
            pl.BlockSpec(memory_space=pl.ANY),
            pl.BlockSpec(memory_space=pl.ANY),
            pl.BlockSpec(memory_space=pl.ANY),
            pl.BlockSpec((tr, d), lambda i: (i, 0)),
            pl.BlockSpec((tr, LANES), lambda i: (i, 0)),
            pl.BlockSpec((1, d), lambda i: (0, 0)),
        ],
        out_specs=pl.BlockSpec((tr, d), lambda i: (i, 0)),
        out_shape=jax.ShapeDtypeStruct((n, d), F32),
        scratch_shapes=[
            pltpu.SMEM((tr,), jnp.int32),
            pltpu.SMEM((tr,), jnp.int32),
            pltpu.VMEM((2, tr, d), F32),
            pltpu.VMEM((2, tr, d), F32),
            pltpu.SemaphoreType.DMA,
            pltpu.SemaphoreType.DMA((2,)),
            pltpu.SemaphoreType.DMA((2,)),
        ],
        compiler_params=_cparams(("arbitrary",)),
        name="moe_combine",
    )(pos0.reshape(n // tr, tr), pos1.reshape(n // tr, tr), ys, x, wt, g.reshape(1, -1))


def _rmsnorm_kernel(x_ref, g_ref, o_ref):
    o_ref[...] = _rms(x_ref[...], g_ref[...])


def _final_norm(x, g, *, tm):
    n, d = x.shape
    return pl.pallas_call(
        _rmsnorm_kernel,
        grid=(n // tm,),
        in_specs=[pl.BlockSpec((tm, d), lambda i: (i, 0)),
                  pl.BlockSpec((1, d), lambda i: (0, 0))],
        out_specs=pl.BlockSpec((tm, d), lambda i: (i, 0)),
        out_shape=jax.ShapeDtypeStruct((n, d), F32),
        compiler_params=_cparams(("arbitrary",)),
        name="final_norm",
    )(x, g.reshape(1, -1))


def _routing_plan(top_idx, *, tm):
    n = top_idx.shape[0]
    n_assign = 2 * n
    rows = n_assign + N_EXPERTS * tm
    n_tiles = rows // tm
    e_flat = top_idx.reshape(n_assign)
    onehot = (e_flat[:, None] == jnp.arange(N_EXPERTS, dtype=jnp.int32)[None, :]).astype(jnp.int32)
    counts = jnp.sum(onehot, axis=0)
    rank = jnp.sum((jnp.cumsum(onehot, axis=0) - onehot) * onehot, axis=1)
    padded = ((counts + tm - 1) // tm) * tm
    ends = jnp.cumsum(padded)
    starts = ends - padded
    dest = starts[e_flat] + rank
    token = jnp.arange(n_assign, dtype=jnp.int32) // 2
    row_token = jnp.zeros((rows,), jnp.int32).at[dest].set(token)
    tile_start = jnp.arange(n_tiles, dtype=jnp.int32) * tm
    tile_expert = jnp.sum((tile_start[:, None] >= ends[None, :]).astype(jnp.int32), axis=1)
    tile_expert = jnp.minimum(tile_expert, N_EXPERTS - 1).astype(jnp.int32)
    n_used = (ends[-1] // tm).astype(jnp.int32).reshape(1)
    pos = dest.reshape(n, 2).astype(jnp.int32)
    return row_token.reshape(n_tiles, tm), tile_expert, n_used, pos


def _moe_layer(x, g, router_pad, wg, wu, wd, final_g, *, final_norm):
    tm = 512
    idx, wt = _router(x, g, router_pad, tm=1024)
    row_token, tile_expert, n_used, pos = _routing_plan(idx[:, :2], tm=tm)
    xs = _gather_norm(n_used, row_token, x, g, tr=tm)
    act = _moe_gate_up(tile_expert, n_used, xs, wg, wu, tm=tm, tn=1024)
    ys = _moe_down(tile_expert, n_used, act, wd, tm=tm, tn=512)
    return _combine(pos[:, 0], pos[:, 1], ys, x, wt, final_g, tr=256, final_norm=final_norm)


def _rope_tables(seq):
    pos = jnp.arange(seq, dtype=F32)[:, None]

    def tab(dim):
        inv = ROPE_THETA ** (-jnp.arange(0, dim, 2, dtype=F32) / dim)
        ang = pos * inv[None, :]
        return jnp.cos(ang), jnp.sin(ang)

    cf, sf = tab(HEAD_DIM)
    cos_f = jnp.concatenate([cf, cf], axis=1)
    sin_f = jnp.concatenate([-sf, sf], axis=1)
    cr, sr = tab(ROPE_DIM)
    zero = jnp.zeros((seq, LANES - ROPE_DIM), F32)
    cos_r = jnp.concatenate([cr, cr, zero], axis=1)
    sin_r = jnp.concatenate([-sr, sr, zero], axis=1)
    return cos_f, sin_f, cos_r, sin_r


def kernel(x, attn_norm_g, w_in, q_norm_g, kv_norm_g, w_uq, w_uk, w_uv, out_a_norm_g,
           out_b_norm_g, w_out, ffn_norm_g, dense_w_gate, dense_w_up, dense_w_down,
           router_w, moe_w_gate, moe_w_up, moe_w_down, final_norm_g):
    batch, seq, d_model = x.shape
    depth = w_in.shape[0]
    n = batch * seq
    mix = DIL_HEADS * HEAD_DIM
    cos_f, sin_f, cos_r, sin_r = _rope_tables(seq)
    dil_bias = _dil_bias(DIL_TQ)

    c0, c1, c2 = Q_LORA, Q_LORA + KV_LORA, Q_LORA + KV_LORA + ROPE_DIM
    lat_tiles = LATENT_COLS // PROJ_TN

    def proj_weight(wi):
        pad = jnp.zeros((d_model, 1024 - Q_LORA - ROPE_DIM), F32)
        return jnp.concatenate([wi[:, :c0], wi[:, c1:c2], pad, wi[:, c0:c1], wi[:, c2:]],
                               axis=1).astype(BF16)

    def q_up_weight(wq):
        uq = wq.reshape(Q_LORA, MLA_HEADS, QK_DIM)
        uq = jnp.concatenate([uq, jnp.zeros((Q_LORA, MLA_HEADS, 2 * LANES - QK_DIM), F32)], axis=2)
        return uq.reshape(Q_LORA, MLA_HEADS * 2 * LANES).astype(BF16)

    def bf16(w, idx):
        return w[idx].astype(BF16)

    xf = x.reshape(n, d_model)
    for layer in range(depth):
        g_attn = attn_norm_g[layer]
        proj = _norm_mm(xf, g_attn, proj_weight(w_in[layer]), cos_f, sin_f, x_block=(d_model, 0),
                        k_norm=d_model, tm=1024, tn=PROJ_TN,
                        rope_tiles=(lat_tiles, lat_tiles + 2 * mix // PROJ_TN),
                        scale_tiles=lat_tiles + mix // PROJ_TN,
                        scale=LOG2_E * HEAD_DIM ** -0.5, name="in_proj")
        q, kr = _q_up(proj, q_norm_g[layer], q_up_weight(w_uq[layer]), cos_r, sin_r, tm=1024,
                      tn=512, scale=LOG2_E * QK_DIM ** -0.5)
        w_ukv = jnp.concatenate([w_uk[layer], w_uv[layer]], axis=1).astype(BF16)
        kv = _norm_mm(proj, kv_norm_g[layer], w_ukv, cos_f, sin_f,
                      x_block=(KV_LORA, 1024 // KV_LORA), k_norm=KV_LORA, tm=1024, tn=1024,
                      name="kv_up")
        o_a = _mla_attention(q, kv, kr, batch=batch, seq=seq, tq=1024, tk=512)
        o_b = _dilated(proj, dil_bias, batch=batch, seq=seq, tq=DIL_TQ, ck=256)
        xf = _out_proj(o_a, o_b, out_a_norm_g[layer], out_b_norm_g[layer],
                       bf16(w_out, layer), xf, tm=1024, tn=512)
        j = layer // 2
        last = layer == depth - 1
        if layer % 2 == 0:
            act = _gate_up(xf, ffn_norm_g[layer], bf16(dense_w_gate, j), bf16(dense_w_up, j),
                           tm=1024, tn=512)
            xf = _down(act, bf16(dense_w_down, j), xf, tm=1024, tn=512)
            if last:
                xf = _final_norm(xf, final_norm_g, tm=1024)
        else:
            router_pad = jnp.concatenate(
                [router_w[j], jnp.zeros((d_model, LANES - N_EXPERTS), F32)], axis=1)
            xf = _moe_layer(xf, ffn_norm_g[layer], router_pad, bf16(moe_w_gate, j),
                            bf16(moe_w_up, j), bf16(moe_w_down, j), final_norm_g,
                            final_norm=last)
    return xf.reshape(batch, seq, d_model)
```

```python
import functools

import jax
import jax.numpy as jnp
from jax import lax
from jax.experimental import pallas as pl
from jax.experimental.pallas import tpu as pltpu

F32 = jnp.float32
BF16 = jnp.bfloat16

LANES = 128
BF16_SUBLANES = 16
MLA_HEADS = 8
DIL_HEADS = 8
HEAD_DIM = 128
Q_LORA = 768
KV_LORA = 512
ROPE_DIM = 64
QK_DIM = HEAD_DIM + ROPE_DIM
N_EXPERTS = 8
DIL_CONFIGS = ((128, 1), (512, 4), (2048, 16))
DIL_PAD = max(w for w, _ in DIL_CONFIGS)
DIL_TQ = 512
LATENT_COLS = 1536
PROJ_TN = 512
ROPE_THETA = 10000.0
EPS = 1e-6
NEG_INF = -1e30
LOG2_E = 1.4426950408889634
VMEM_LIMIT = 56 * 1024 * 1024


def _cparams(sem):
    return pltpu.CompilerParams(dimension_semantics=sem, vmem_limit_bytes=VMEM_LIMIT)


def _rms(xf, g):
    ms = jnp.mean(xf * xf, axis=-1, keepdims=True)
    return xf * lax.rsqrt(ms + EPS) * g


def _rope_full(a, c, s):
    return a * c + pltpu.roll(a, HEAD_DIM // 2, 1) * s


def _rope_half(a, c, s):
    lane = lax.broadcasted_iota(jnp.int32, a.shape, 1)
    lower = (lane % ROPE_DIM) < (ROPE_DIM // 2)
    partner = jnp.where(lower, pltpu.roll(a, LANES - ROPE_DIM // 2, 1),
                        pltpu.roll(a, ROPE_DIM // 2, 1))
    return a * c + partner * s


def _norm_mm_kernel(x_ref, g_ref, w_ref, cos_ref, sin_ref, o_ref, h_ref, *,
                    k_norm, rope_tiles, scale_tiles, scale):
    j = pl.program_id(1)

    @pl.when(j == 0)
    def _():
        xf = x_ref[:, :k_norm].astype(F32)
        h_ref[...] = _rms(xf, g_ref[...]).astype(BF16)

    acc = jnp.dot(h_ref[...], w_ref[...], preferred_element_type=F32)
    if rope_tiles is None:
        o_ref[...] = acc.astype(o_ref.dtype)
        return
    rope = (j >= rope_tiles[0]) & (j < rope_tiles[1])

    @pl.when(rope)
    def _():
        scaled = j < scale_tiles
        c = jnp.where(scaled, cos_ref[...] * scale, cos_ref[...])
        s = jnp.where(scaled, sin_ref[...] * scale, sin_ref[...])
        for k in range(acc.shape[1] // HEAD_DIM):
            sl = slice(k * HEAD_DIM, (k + 1) * HEAD_DIM)
            o_ref[:, sl] = _rope_full(acc[:, sl], c, s).astype(o_ref.dtype)

    @pl.when(jnp.logical_not(rope))
    def _():
        o_ref[...] = acc.astype(o_ref.dtype)


def _norm_mm(x, g, w, cos, sin, *, x_block, k_norm, tm, tn, rope_tiles=None, scale_tiles=0,
             scale=1.0, name):
    n = x.shape[0]
    xw, xb = x_block
    kw, nout = w.shape
    assert kw == k_norm and n % tm == 0 and nout % tn == 0
    ns = cos.shape[0] // tm
    kern = functools.partial(_norm_mm_kernel, k_norm=k_norm, rope_tiles=rope_tiles,
                             scale_tiles=scale_tiles, scale=scale)
    return pl.pallas_call(
        kern,
        grid=(n // tm, nout // tn),
        in_specs=[
            pl.BlockSpec((tm, xw), lambda i, j: (i, xb)),
            pl.BlockSpec((1, k_norm), lambda i, j: (0, 0)),
            pl.BlockSpec((k_norm, tn), lambda i, j: (0, j)),
            pl.BlockSpec((tm, HEAD_DIM), lambda i, j: (i % ns, 0)),
            pl.BlockSpec((tm, HEAD_DIM), lambda i, j: (i % ns, 0)),
        ],
        out_specs=pl.BlockSpec((tm, tn), lambda i, j: (i, j)),
        out_shape=jax.ShapeDtypeStruct((n, nout), BF16),
        scratch_shapes=[pltpu.VMEM((tm, k_norm), BF16)],
        compiler_params=_cparams(("arbitrary", "arbitrary")),
        name=name,
    )(x, g.reshape(1, -1), w, cos, sin)


def _q_up_kernel(x_ref, g_ref, w_ref, cos_ref, sin_ref, q_ref, kr_ref, h_ref, *, scale):
    j = pl.program_id(1)
    c = cos_ref[...]
    s = sin_ref[...]

    @pl.when(j == 0)
    def _():
        blk = x_ref[...].astype(F32)
        h_ref[...] = _rms(blk[:, :Q_LORA], g_ref[...]).astype(BF16)
        kr_ref[...] = _rope_half(blk[:, Q_LORA:Q_LORA + LANES], c, s).astype(BF16)

    acc = jnp.dot(h_ref[...], w_ref[...], preferred_element_type=F32)
    for hh in range(acc.shape[1] // (2 * LANES)):
        lo = hh * 2 * LANES
        q_ref[:, lo:lo + LANES] = (acc[:, lo:lo + LANES] * scale).astype(BF16)
        q_ref[:, lo + LANES:lo + 2 * LANES] = (
            _rope_half(acc[:, lo + LANES:lo + 2 * LANES], c, s) * scale).astype(BF16)


def _q_up(lat, g, w, cos, sin, *, tm, tn, scale):
    n = lat.shape[0]
    nout = w.shape[1]
    ns = cos.shape[0] // tm
    return pl.pallas_call(
        functools.partial(_q_up_kernel, scale=scale),
        grid=(n // tm, nout // tn),
        in_specs=[
            pl.BlockSpec((tm, 1024), lambda i, j: (i, 0)),
            pl.BlockSpec((1, Q_LORA), lambda i, j: (0, 0)),
            pl.BlockSpec((Q_LORA, tn), lambda i, j: (0, j)),
            pl.BlockSpec((tm, LANES), lambda i, j: (i % ns, 0)),
            pl.BlockSpec((tm, LANES), lambda i, j: (i % ns, 0)),
        ],
        out_specs=[
            pl.BlockSpec((tm, tn), lambda i, j: (i, j)),
            pl.BlockSpec((tm, LANES), lambda i, j: (i, 0)),
        ],
        out_shape=[jax.ShapeDtypeStruct((n, nout), BF16),
                   jax.ShapeDtypeStruct((n, LANES), BF16)],
        scratch_shapes=[pltpu.VMEM((tm, Q_LORA), BF16)],
        compiler_params=_cparams(("arbitrary", "arbitrary")),
        name="q_up",
    )(lat, g.reshape(1, -1), w, cos, sin)


def _mla_kernel(q_ref, kn_ref, kr_ref, v_ref, o_ref, kcat_ref, vt_ref, qt_ref, m_ref,
                acc_ref, s_ref, *, tq, tk):
    qi = pl.program_id(2)
    seq = kcat_ref.shape[0]
    vt_rows = vt_ref.shape[1]

    @pl.when(qi == 0)
    def _():
        kcat_ref[:, :HEAD_DIM] = kn_ref[...]
        kcat_ref[:, HEAD_DIM:] = kr_ref[...]
        ones_row = (lax.broadcasted_iota(jnp.int32, (vt_rows - HEAD_DIM, tk), 0) == 0)

        def transpose_chunk(c, carry):
            off = pl.multiple_of(c * tk, tk)
            vt_ref[c, :HEAD_DIM, :] = v_ref[pl.ds(off, tk), :].astype(F32).T.astype(BF16)
            vt_ref[c, HEAD_DIM:, :] = ones_row.astype(BF16)
            return carry

        lax.fori_loop(0, seq // tk, transpose_chunk, 0)

    qt_ref[...] = q_ref[...].astype(F32).T.astype(BF16)
    m_ref[...] = jnp.full(m_ref.shape, NEG_INF, F32)
    acc_ref[...] = jnp.zeros(acc_ref.shape, F32)
    per_q = tq // tk

    def scores(j):
        off = pl.multiple_of(j * tk, tk)
        return jnp.dot(kcat_ref[pl.ds(off, tk), :], qt_ref[...],
                       preferred_element_type=F32)

    def chunk(j, masked, has_next):
        s = s_ref[j % 2]
        if has_next:
            s_ref[(j + 1) % 2] = scores(j + 1)
        if masked:
            key = j * tk + lax.broadcasted_iota(jnp.int32, (tk, tq), 0)
            qry = qi * tq + lax.broadcasted_iota(jnp.int32, (tk, tq), 1)
            s = jnp.where(key <= qry, s, NEG_INF)
        m_old = m_ref[...]
        m_new = jnp.maximum(m_old, jnp.max(s, axis=0, keepdims=True))
        a = jnp.exp2(m_old - m_new)
        p = jnp.exp2(s - m_new).astype(BF16)
        acc_ref[...] = a * acc_ref[...] + jnp.dot(vt_ref[j], p, preferred_element_type=F32)
        m_ref[...] = m_new

    def body(j, carry):
        chunk(j, False, True)
        return carry

    n_full = qi * per_q
    s_ref[0] = scores(0)
    lax.fori_loop(0, n_full, body, 0)
    for d in range(per_q):
        chunk(n_full + d, True, d + 1 < per_q)
    out_t = acc_ref[:HEAD_DIM, :] / acc_ref[HEAD_DIM:HEAD_DIM + 1, :]
    o_ref[...] = out_t.T.astype(o_ref.dtype)


def _mla_attention(q, kv, kr, *, batch, seq, tq, tk):
    n = q.shape[0]
    nq = seq // tq
    return pl.pallas_call(
        functools.partial(_mla_kernel, tq=tq, tk=tk),
        grid=(batch, MLA_HEADS, nq),
        in_specs=[
            pl.BlockSpec((tq, 2 * LANES), lambda b, h, i: (b * nq + i, h)),
            pl.BlockSpec((seq, HEAD_DIM), lambda b, h, i: (b, h)),
            pl.BlockSpec((seq, LANES), lambda b, h, i: (b, 0)),
            pl.BlockSpec((seq, HEAD_DIM), lambda b, h, i: (b, MLA_HEADS + h)),
        ],
        out_specs=pl.BlockSpec((tq, HEAD_DIM), lambda b, h, i: (b * nq + i, h)),
        out_shape=jax.ShapeDtypeStruct((n, MLA_HEADS * HEAD_DIM), BF16),
        scratch_shapes=[
            pltpu.VMEM((seq, 2 * LANES), BF16),
            pltpu.VMEM((seq // tk, HEAD_DIM + BF16_SUBLANES, tk), BF16),
            pltpu.VMEM((2 * LANES, tq), BF16),
            pltpu.VMEM((1, tq), F32),
            pltpu.VMEM((HEAD_DIM + BF16_SUBLANES, tq), F32),
            pltpu.VMEM((2, tk, tq), F32),
        ],
        compiler_params=_cparams(("arbitrary", "arbitrary", "arbitrary")),
        name="mla_attention",
    )(q, kv, kr, kv)


def _dil_bias(tq):
    kk = jnp.arange(DIL_PAD + tq, dtype=jnp.int32)[:, None]
    qq = jnp.arange(tq, dtype=jnp.int32)[None, :]
    delta = qq + DIL_PAD - kk
    count = jnp.zeros(delta.shape, F32)
    for win, dil in DIL_CONFIGS:
        count += ((delta >= 0) & (delta <= win) & (delta % dil == 0)).astype(F32)
    return jnp.where(count > 0, jnp.log2(jnp.maximum(count, 1.0)), NEG_INF)


def _dil_kernel(q_ref, k_ref, v_ref, bias_ref, o_ref, kpad_ref, vt_ref, s_ref, *, tq):
    i = pl.program_id(2)
    seq = k_ref.shape[0]
    ck = vt_ref.shape[2]
    vt_rows = vt_ref.shape[1]
    n_pad = DIL_PAD // ck
    win = DIL_PAD + tq

    @pl.when(i == 0)
    def _():
        kpad_ref[:DIL_PAD, :] = jnp.zeros((DIL_PAD, HEAD_DIM), BF16)
        kpad_ref[DIL_PAD:, :] = k_ref[...]
        ones_row = (lax.broadcasted_iota(jnp.int32, (vt_rows - HEAD_DIM, ck), 0) == 0)
        for c in range(n_pad):
            vt_ref[c] = jnp.zeros((vt_rows, ck), BF16)

        def transpose_chunk(c, carry):
            off = pl.multiple_of(c * ck, ck)
            vt_ref[n_pad + c, :HEAD_DIM, :] = v_ref[pl.ds(off, ck), :].astype(F32).T.astype(BF16)
            vt_ref[n_pad + c, HEAD_DIM:, :] = ones_row.astype(BF16)
            return carry

        lax.fori_loop(0, seq // ck, transpose_chunk, 0)

    s0 = pl.multiple_of(i * tq, tq)
    qt = q_ref[...].astype(F32).T.astype(BF16)
    s_ref[...] = jnp.dot(kpad_ref[pl.ds(s0, win), :], qt,
                         preferred_element_type=F32) + bias_ref[...]

    @pl.when(s0 < DIL_PAD)
    def _():
        row = lax.broadcasted_iota(jnp.int32, (win, tq), 0)
        s_ref[...] = jnp.where(row >= DIL_PAD - s0, s_ref[...], NEG_INF)

    s = s_ref[...]
    m = jnp.max(s, axis=0, keepdims=True)
    p = jnp.exp2(s - m).astype(BF16)
    base = i * (tq // ck)
    acc = jnp.zeros((vt_rows, tq), F32)
    for c in range(win // ck):
        acc += jnp.dot(vt_ref[base + c], p[c * ck:(c + 1) * ck, :], preferred_element_type=F32)
    o_ref[...] = (acc[:HEAD_DIM, :] / acc[HEAD_DIM:HEAD_DIM + 1, :]).T


def _dilated(qkv, bias, *, batch, seq, tq, ck):
    n = qkv.shape[0]
    nq = seq // tq
    win = DIL_PAD + tq
    vt_rows = HEAD_DIM + BF16_SUBLANES
    q0 = LATENT_COLS // HEAD_DIM
    return pl.pallas_call(
        functools.partial(_dil_kernel, tq=tq),
        grid=(batch, DIL_HEADS, nq),
        in_specs=[
            pl.BlockSpec((tq, HEAD_DIM), lambda b, h, i: (b * nq + i, q0 + h)),
            pl.BlockSpec((seq, HEAD_DIM), lambda b, h, i: (b, q0 + DIL_HEADS + h)),
            pl.BlockSpec((seq, HEAD_DIM), lambda b, h, i: (b, q0 + 2 * DIL_HEADS + h)),
            pl.BlockSpec((win, tq), lambda b, h, i: (0, 0)),
        ],
        out_specs=pl.BlockSpec((tq, HEAD_DIM), lambda b, h, i: (b * nq + i, h)),
        out_shape=jax.ShapeDtypeStruct((n, DIL_HEADS * HEAD_DIM), F32),
        scratch_shapes=[
            pltpu.VMEM((DIL_PAD + seq, HEAD_DIM), BF16),
            pltpu.VMEM(((DIL_PAD + seq) // ck, vt_rows, ck), BF16),
            pltpu.VMEM((win, tq), F32),
        ],
        compiler_params=_cparams(("arbitrary", "arbitrary", "arbitrary")),
        name="dilated_attention",
    )(qkv, qkv, qkv, bias)


def _out_proj_kernel(oa_ref, ob_ref, ga_ref, gb_ref, w_ref, x_ref, o_ref, h_ref):
    half = oa_ref.shape[1]

    @pl.when(pl.program_id(1) == 0)
    def _():
        h_ref[:, :half] = _rms(oa_ref[...].astype(F32), ga_ref[...]).astype(BF16)
        h_ref[:, half:] = _rms(ob_ref[...], gb_ref[...]).astype(BF16)

    o_ref[...] = x_ref[...] + jnp.dot(h_ref[...], w_ref[...], preferred_element_type=F32)


def _out_proj(o_a, o_b, g_a, g_b, w, x, *, tm, tn):
    n, d = x.shape
    half = o_a.shape[1]
    row = lambda i, j: (i, 0)
    return pl.pallas_call(
        _out_proj_kernel,
        grid=(n // tm, d // tn),
        in_specs=[pl.BlockSpec((tm, half), row)] * 2
        + [pl.BlockSpec((1, half), lambda i, j: (0, 0))] * 2
        + [pl.BlockSpec((2 * half, tn), lambda i, j: (0, j)),
           pl.BlockSpec((tm, tn), lambda i, j: (i, j))],
        out_specs=pl.BlockSpec((tm, tn), lambda i, j: (i, j)),
        out_shape=jax.ShapeDtypeStruct((n, d), F32),
        scratch_shapes=[pltpu.VMEM((tm, 2 * half), BF16)],
        compiler_params=_cparams(("arbitrary", "arbitrary")),
        name="out_proj",
    )(o_a, o_b, g_a.reshape(1, -1), g_b.reshape(1, -1), w, x)


def _gate_up_kernel(x_ref, g_ref, wg_ref, wu_ref, o_ref, h_ref):
    @pl.when(pl.program_id(1) == 0)
    def _():
        h_ref[...] = _rms(x_ref[...], g_ref[...]).astype(BF16)

    h = h_ref[...]
    gate = jnp.dot(h, wg_ref[...], preferred_element_type=F32)
    up = jnp.dot(h, wu_ref[...], preferred_element_type=F32)
    o_ref[...] = (gate * jax.nn.sigmoid(gate) * up).astype(o_ref.dtype)


def _gate_up(x, g, wg, wu, *, tm, tn):
    n, d = x.shape
    f = wg.shape[1]
    return pl.pallas_call(
        _gate_up_kernel,
        grid=(n // tm, f // tn),
        in_specs=[
            pl.BlockSpec((tm, d), lambda i, j: (i, 0)),
            pl.BlockSpec((1, d), lambda i, j: (0, 0)),
            pl.BlockSpec((d, tn), lambda i, j: (0, j)),
            pl.BlockSpec((d, tn), lambda i, j: (0, j)),
        ],
        out_specs=pl.BlockSpec((tm, tn), lambda i, j: (i, j)),
        out_shape=jax.ShapeDtypeStruct((n, f), BF16),
        scratch_shapes=[pltpu.VMEM((tm, d), BF16)],
        compiler_params=_cparams(("arbitrary", "arbitrary")),
        name="ffn_gate_up",
    )(x, g.reshape(1, -1), wg, wu)


def _down_kernel(a_ref, w_ref, x_ref, o_ref):
    o_ref[...] = x_ref[...] + jnp.dot(a_ref[...], w_ref[...], preferred_element_type=F32)


def _down(act, w, x, *, tm, tn):
    n, f = act.shape
    d = w.shape[1]
    return pl.pallas_call(
        _down_kernel,
        grid=(n // tm, d // tn),
        in_specs=[
            pl.BlockSpec((tm, f), lambda i, j: (i, 0)),
            pl.BlockSpec((f, tn), lambda i, j: (0, j)),
            pl.BlockSpec((tm, tn), lambda i, j: (i, j)),
        ],
        out_specs=pl.BlockSpec((tm, tn), lambda i, j: (i, j)),
        out_shape=jax.ShapeDtypeStruct((n, d), F32),
        compiler_params=_cparams(("arbitrary", "arbitrary")),
        name="ffn_down",
    )(act, w, x)


def _router_kernel(x_ref, g_ref, w_ref, idx_ref, wt_ref):
    h = _rms(x_ref[...], g_ref[...])
    logits = jnp.dot(h, w_ref[...], preferred_element_type=F32,
                     precision=lax.Precision.HIGHEST)
    lane_i = lax.broadcasted_iota(jnp.int32, logits.shape, 1)
    lane = lane_i.astype(F32)
    logits = jnp.where(lane_i < N_EXPERTS, logits, -jnp.inf)
    v1 = jnp.max(logits, axis=-1, keepdims=True)
    i1 = jnp.min(jnp.where(logits == v1, lane, float(LANES)), axis=-1, keepdims=True)
    rest = jnp.where(lane == i1, -jnp.inf, logits)
    v2 = jnp.max(rest, axis=-1, keepdims=True)
    i2 = jnp.min(jnp.where(rest == v2, lane, float(LANES)), axis=-1, keepdims=True)
    e2 = jnp.exp(v2 - v1)
    w1 = 1.0 / (1.0 + e2)
    w2 = e2 / (1.0 + e2)
    idx_ref[...] = jnp.where(lane_i == 0, i1, jnp.where(lane_i == 1, i2, 0.0)).astype(jnp.int32)
    wt_ref[...] = jnp.where(lane_i == 0, w1, jnp.where(lane_i == 1, w2, 0.0))


def _router(x, g, w_pad, *, tm):
    n, d = x.shape
    return pl.pallas_call(
        _router_kernel,
        grid=(n // tm,),
        in_specs=[
            pl.BlockSpec((tm, d), lambda i: (i, 0)),
            pl.BlockSpec((1, d), lambda i: (0, 0)),
            pl.BlockSpec((d, LANES), lambda i: (0, 0)),
        ],
        out_specs=[pl.BlockSpec((tm, LANES), lambda i: (i, 0))] * 2,
        out_shape=[jax.ShapeDtypeStruct((n, LANES), jnp.int32),
                   jax.ShapeDtypeStruct((n, LANES), F32)],
        compiler_params=_cparams(("arbitrary",)),
        name="moe_router",
    )(x, g.reshape(1, -1), w_pad)


def _row_copy(src_hbm, row, dst_vmem, slot, sem):
    return pltpu.make_async_copy(src_hbm.at[pl.ds(row, 1)], dst_vmem.at[pl.ds(slot, 1)], sem)


DMA_UNROLL = 8
DMA_PRIORITIES = 2


def _load_indices(idx_hbm, tile, idx_smem, isem):
    cp = pltpu.make_async_copy(idx_hbm.at[tile], idx_smem, isem)
    cp.start()
    cp.wait()


def _start_rows(idx_smem, src_hbm, dst_vmem, sem):
    def start(i, c):
        for prio in range(DMA_PRIORITIES):
            row = DMA_PRIORITIES * i + prio
            _row_copy(src_hbm, idx_smem[row], dst_vmem, row, sem).start(priority=prio)
        return c

    lax.fori_loop(0, dst_vmem.shape[0] // DMA_PRIORITIES, start, 0,
                  unroll=DMA_UNROLL // DMA_PRIORITIES)


def _wait_rows(src_hbm, dst_vmem, sem):
    def wait(i, c):
        _row_copy(src_hbm, 0, dst_vmem, i, sem).wait()
        return c

    lax.fori_loop(0, dst_vmem.shape[0], wait, 0, unroll=DMA_UNROLL)


def _gather_norm_kernel(nu_ref, idx_hbm, x_hbm, g_ref, o_ref, idx_smem, buf, isem, sem):
    i = pl.program_id(0)
    n_used = nu_ref[0]

    def fetch(tile, slot):
        _load_indices(idx_hbm, tile, idx_smem, isem)
        _start_rows(idx_smem, x_hbm, buf.at[slot], sem.at[slot])

    @pl.when((i == 0) & (n_used > 0))
    def _():
        fetch(0, 0)

    @pl.when(i + 1 < n_used)
    def _():
        fetch(i + 1, (i + 1) % 2)

    @pl.when(i < n_used)
    def _():
        slot = i % 2
        _wait_rows(x_hbm, buf.at[slot], sem.at[slot])
        o_ref[...] = _rms(buf[slot], g_ref[...]).astype(o_ref.dtype)

    @pl.when(i >= n_used)
    def _():
        o_ref[...] = jnp.zeros(o_ref.shape, o_ref.dtype)


def _gather_norm(n_used, row_token, x, g, *, tr):
    nt = row_token.shape[0]
    d = x.shape[1]
    grid_spec = pltpu.PrefetchScalarGridSpec(
        num_scalar_prefetch=1,
        grid=(nt,),
        in_specs=[
            pl.BlockSpec(memory_space=pl.ANY),
            pl.BlockSpec(memory_space=pl.ANY),
            pl.BlockSpec((1, d), lambda i, nu: (0, 0)),
        ],
        out_specs=pl.BlockSpec((tr, d), lambda i, nu: (i, 0)),
        scratch_shapes=[
            pltpu.SMEM((tr,), jnp.int32),
            pltpu.VMEM((2, tr, d), F32),
            pltpu.SemaphoreType.DMA,
            pltpu.SemaphoreType.DMA((2,)),
        ],
    )
    return pl.pallas_call(
        _gather_norm_kernel,
        grid_spec=grid_spec,
        out_shape=jax.ShapeDtypeStruct((nt * tr, d), BF16),
        compiler_params=_cparams(("arbitrary",)),
        name="moe_gather",
    )(n_used, row_token, x, g.reshape(1, -1))


def _moe_gate_up_kernel(te_ref, nu_ref, x_ref, wg_ref, wu_ref, o_ref, wg_bf, wu_bf):
    i = pl.program_id(1)
    used = i < nu_ref[0]
    fresh = (i == 0) | (te_ref[i] != te_ref[jnp.maximum(i - 1, 0)])

    @pl.when(used & fresh)
    def _():
        wg_bf[...] = wg_ref[0, 0].astype(BF16)
        wu_bf[...] = wu_ref[0, 0].astype(BF16)

    @pl.when(used)
    def _():
        h = x_ref[...]
        gate = jnp.dot(h, wg_bf[...], preferred_element_type=F32)
        up = jnp.dot(h, wu_bf[...], preferred_element_type=F32)
        o_ref[...] = (gate * jax.nn.sigmoid(gate) * up).astype(o_ref.dtype)

    @pl.when(jnp.logical_not(used))
    def _():
        o_ref[...] = jnp.zeros(o_ref.shape, o_ref.dtype)


def _moe_gate_up(tile_expert, n_used, xs, wg, wu, layer, *, tm, tn):
    r, d = xs.shape
    f = wg.shape[3]
    w_spec = pl.BlockSpec((1, 1, d, tn), lambda j, i, te, nu: (layer, te[i], 0, j))
    grid_spec = pltpu.PrefetchScalarGridSpec(
        num_scalar_prefetch=2,
        grid=(f // tn, r // tm),
        in_specs=[pl.BlockSpec((tm, d), lambda j, i, te, nu: (i, 0)), w_spec, w_spec],
        out_specs=pl.BlockSpec((tm, tn), lambda j, i, te, nu: (i, j)),
        scratch_shapes=[pltpu.VMEM((d, tn), BF16), pltpu.VMEM((d, tn), BF16)],
    )
    return pl.pallas_call(
        _moe_gate_up_kernel,
        grid_spec=grid_spec,
        out_shape=jax.ShapeDtypeStruct((r, f), BF16),
        compiler_params=_cparams(("arbitrary", "arbitrary")),
        name="moe_gate_up",
    )(tile_expert, n_used, xs, wg, wu)


def _moe_down_kernel(te_ref, nu_ref, a_ref, w_ref, o_ref):
    i = pl.program_id(1)

    @pl.when(i < nu_ref[0])
    def _():
        o_ref[...] = jnp.dot(a_ref[...], w_ref[0], preferred_element_type=F32)

    @pl.when(i >= nu_ref[0])
    def _():
        o_ref[...] = jnp.zeros(o_ref.shape, o_ref.dtype)


def _moe_down(tile_expert, n_used, act, w, layer, *, tm, tn):
    r, f = act.shape
    d = w.shape[3]
    grid_spec = pltpu.PrefetchScalarGridSpec(
        num_scalar_prefetch=2,
        grid=(d // tn, r // tm),
        in_specs=[
            pl.BlockSpec((tm, f), lambda j, i, te, nu: (i, 0)),
            pl.BlockSpec((None, 1, f, tn), lambda j, i, te, nu: (layer, te[i], 0, j)),
        ],
        out_specs=pl.BlockSpec((tm, tn), lambda j, i, te, nu: (i, j)),
    )
    return pl.pallas_call(
        _moe_down_kernel,
        grid_spec=grid_spec,
        out_shape=jax.ShapeDtypeStruct((r, d), F32),
        compiler_params=_cparams(("arbitrary", "arbitrary")),
        name="moe_down",
    )(tile_expert, n_used, act, w)


def _combine_kernel(p0_hbm, p1_hbm, ys_hbm, x_ref, wt_ref, g_ref, o_ref, i0_smem, i1_smem,
                    b0, b1, isem, sem0, sem1, *, final_norm):
    i = pl.program_id(0)
    n = pl.num_programs(0)

    def fetch(tile, slot):
        _load_indices(p0_hbm, tile, i0_smem, isem)
        _load_indices(p1_hbm, tile, i1_smem, isem)
        _start_rows(i0_smem, ys_hbm, b0.at[slot], sem0.at[slot])
        _start_rows(i1_smem, ys_hbm, b1.at[slot], sem1.at[slot])

    @pl.when(i == 0)
    def _():
        fetch(0, 0)

    @pl.when(i + 1 < n)
    def _():
        fetch(i + 1, (i + 1) % 2)

    slot = i % 2
    _wait_rows(ys_hbm, b0.at[slot], sem0.at[slot])
    _wait_rows(ys_hbm, b1.at[slot], sem1.at[slot])
    wt = wt_ref[...]
    y = x_ref[...] + (wt[:, 0:1] * b0[slot] + wt[:, 1:2] * b1[slot])
    if final_norm:
        y = _rms(y, g_ref[...])
    o_ref[...] = y


def _combine(pos0, pos1, ys, x, wt, g, *, tr, final_norm):
    n, d = x.shape
    return pl.pallas_call(
        functools.partial(_combine_kernel, final_norm=final_norm),
        grid=(n // tr,),
        in_specs=[
            pl.BlockSpec(memory_space=pl.ANY),
            pl.BlockSpec(memory_space=pl.ANY),
            pl.BlockSpec(memory_space=pl.ANY),
            pl.BlockSpec((tr, d), lambda i: (i, 0)),
            pl.BlockSpec((tr, LANES), lambda i: (i, 0)),
            pl.BlockSpec((1, d), lambda i: (0, 0)),
        ],
        out_specs=pl.BlockSpec((tr, d), lambda i: (i, 0)),
        out_shape=jax.ShapeDtypeStruct((n, d), F32),
        scratch_shapes=[
            pltpu.SMEM((tr,), jnp.int32),
            pltpu.SMEM((tr,), jnp.int32),
            pltpu.VMEM((2, tr, d), F32),
            pltpu.VMEM((2, tr, d), F32),
            pltpu.SemaphoreType.DMA,
            pltpu.SemaphoreType.DMA((2,)),
            pltpu.SemaphoreType.DMA((2,)),
        ],
        compiler_params=_cparams(("arbitrary",)),
        name="moe_combine",
    )(pos0.reshape(n // tr, tr), pos1.reshape(n // tr, tr), ys, x, wt, g.reshape(1, -1))


def _rmsnorm_kernel(x_ref, g_ref, o_ref):
    o_ref[...] = _rms(x_ref[...], g_ref[...])


def _final_norm(x, g, *, tm):
    n, d = x.shape
    return pl.pallas_call(
        _rmsnorm_kernel,
        grid=(n // tm,),
        in_specs=[pl.BlockSpec((tm, d), lambda i: (i, 0)),
                  pl.BlockSpec((1, d), lambda i: (0, 0))],
        out_specs=pl.BlockSpec((tm, d), lambda i: (i, 0)),
        out_shape=jax.ShapeDtypeStruct((n, d), F32),
        compiler_params=_cparams(("arbitrary",)),
        name="final_norm",
    )(x, g.reshape(1, -1))


def _routing_plan(top_idx, *, tm):
    n = top_idx.shape[0]
    n_assign = 2 * n
    rows = n_assign + N_EXPERTS * tm
    n_tiles = rows // tm
    e_flat = top_idx.reshape(n_assign)
    onehot = (e_flat[:, None] == jnp.arange(N_EXPERTS, dtype=jnp.int32)[None, :]).astype(jnp.int32)
    counts = jnp.sum(onehot, axis=0)
    rank = jnp.sum((jnp.cumsum(onehot, axis=0) - onehot) * onehot, axis=1)
    padded = ((counts + tm - 1) // tm) * tm
    ends = jnp.cumsum(padded)
    starts = ends - padded
    dest = starts[e_flat] + rank
    token = jnp.arange(n_assign, dtype=jnp.int32) // 2
    row_token = jnp.zeros((rows,), jnp.int32).at[dest].set(token)
    tile_start = jnp.arange(n_tiles, dtype=jnp.int32) * tm
    tile_expert = jnp.sum((tile_start[:, None] >= ends[None, :]).astype(jnp.int32), axis=1)
    tile_expert = jnp.minimum(tile_expert, N_EXPERTS - 1).astype(jnp.int32)
    n_used = (ends[-1] // tm).astype(jnp.int32).reshape(1)
    pos = dest.reshape(n, 2).astype(jnp.int32)
    return row_token.reshape(n_tiles, tm), tile_expert, n_used, pos


def _moe_layer(x, g, router_pad, wg, wu, wd, layer, final_g, *, final_norm):
    tm = 512
    idx, wt = _router(x, g, router_pad, tm=1024)
    row_token, tile_expert, n_used, pos = _routing_plan(idx[:, :2], tm=tm)
    xs = _gather_norm(n_used, row_token, x, g, tr=tm)
    act = _moe_gate_up(tile_expert, n_used, xs, wg, wu, layer, tm=tm, tn=512)
    ys = _moe_down(tile_expert, n_used, act, wd, layer, tm=tm, tn=1024)
    return _combine(pos[:, 0], pos[:, 1], ys, x, wt, final_g, tr=256, final_norm=final_norm)


def _rope_tables(seq):
    pos = jnp.arange(seq, dtype=F32)[:, None]

    def tab(dim):
        inv = ROPE_THETA ** (-jnp.arange(0, dim, 2, dtype=F32) / dim)
        ang = pos * inv[None, :]
        return jnp.cos(ang), jnp.sin(ang)

    cf, sf = tab(HEAD_DIM)
    cos_f = jnp.concatenate([cf, cf], axis=1)
    sin_f = jnp.concatenate([-sf, sf], axis=1)
    cr, sr = tab(ROPE_DIM)
    zero = jnp.zeros((seq, LANES - ROPE_DIM), F32)
    cos_r = jnp.concatenate([cr, cr, zero], axis=1)
    sin_r = jnp.concatenate([-sr, sr, zero], axis=1)
    return cos_f, sin_f, cos_r, sin_r


def kernel(x, attn_norm_g, w_in, q_norm_g, kv_norm_g, w_uq, w_uk, w_uv, out_a_norm_g,
           out_b_norm_g, w_out, ffn_norm_g, dense_w_gate, dense_w_up, dense_w_down,
           router_w, moe_w_gate, moe_w_up, moe_w_down, final_norm_g):
    batch, seq, d_model = x.shape
    depth = w_in.shape[0]
    n = batch * seq
    mix = DIL_HEADS * HEAD_DIM
    cos_f, sin_f, cos_r, sin_r = _rope_tables(seq)
    dil_bias = _dil_bias(DIL_TQ)

    c0, c1, c2 = Q_LORA, Q_LORA + KV_LORA, Q_LORA + KV_LORA + ROPE_DIM
    lat_tiles = LATENT_COLS // PROJ_TN

    def proj_weight(wi):
        pad = jnp.zeros((d_model, 1024 - Q_LORA - ROPE_DIM), F32)
        return jnp.concatenate([wi[:, :c0], wi[:, c1:c2], pad, wi[:, c0:c1], wi[:, c2:]],
                               axis=1).astype(BF16)

    def q_up_weight(wq):
        uq = wq.reshape(Q_LORA, MLA_HEADS, QK_DIM)
        uq = jnp.concatenate([uq, jnp.zeros((Q_LORA, MLA_HEADS, 2 * LANES - QK_DIM), F32)], axis=2)
        return uq.reshape(Q_LORA, MLA_HEADS * 2 * LANES).astype(BF16)

    def bf16(w, idx):
        return w[idx].astype(BF16)

    moe_down_bf = moe_w_down.astype(BF16)

    xf = x.reshape(n, d_model)
    for layer in range(depth):
        g_attn = attn_norm_g[layer]
        proj = _norm_mm(xf, g_attn, proj_weight(w_in[layer]), cos_f, sin_f, x_block=(d_model, 0),
                        k_norm=d_model, tm=1024, tn=PROJ_TN,
                        rope_tiles=(lat_tiles, lat_tiles + 2 * mix // PROJ_TN),
                        scale_tiles=lat_tiles + mix // PROJ_TN,
                        scale=LOG2_E * HEAD_DIM ** -0.5, name="in_proj")
        q, kr = _q_up(proj, q_norm_g[layer], q_up_weight(w_uq[layer]), cos_r, sin_r, tm=1024,
                      tn=512, scale=LOG2_E * QK_DIM ** -0.5)
        w_ukv = jnp.concatenate([w_uk[layer], w_uv[layer]], axis=1).astype(BF16)
        kv = _norm_mm(proj, kv_norm_g[layer], w_ukv, cos_f, sin_f,
                      x_block=(KV_LORA, 1024 // KV_LORA), k_norm=KV_LORA, tm=1024, tn=1024,
                      name="kv_up")
        o_a = _mla_attention(q, kv, kr, batch=batch, seq=seq, tq=1024, tk=512)
        o_b = _dilated(proj, dil_bias, batch=batch, seq=seq, tq=DIL_TQ, ck=256)
        xf = _out_proj(o_a, o_b, out_a_norm_g[layer], out_b_norm_g[layer],
                       bf16(w_out, layer), xf, tm=1024, tn=512)
        j = layer // 2
        last = layer == depth - 1
        if layer % 2 == 0:
            act = _gate_up(xf, ffn_norm_g[layer], bf16(dense_w_gate, j), bf16(dense_w_up, j),
                           tm=1024, tn=512)
            xf = _down(act, bf16(dense_w_down, j), xf, tm=1024, tn=512)
            if last:
                xf = _final_norm(xf, final_norm_g, tm=1024)
        else:
            router_pad = jnp.concatenate(
                [router_w[j], jnp.zeros((d_model, LANES - N_EXPERTS), F32)], axis=1)
            xf = _moe_layer(xf, ffn_norm_g[layer], router_pad, moe_w_gate, moe_w_up, moe_down_bf,
                            j, final_norm_g, final_norm=last)
    return xf.reshape(batch, seq, d_model)
```

```python
import functools

import jax
import jax.numpy as jnp
from jax import lax
from jax.experimental import pallas as pl
from jax.experimental.pallas import tpu as pltpu

F32 = jnp.float32
BF16 = jnp.bfloat16

LANES = 128
BF16_SUBLANES = 16
MLA_HEADS = 8
DIL_HEADS = 8
HEAD_DIM = 128
Q_LORA = 768
KV_LORA = 512
ROPE_DIM = 64
QK_DIM = HEAD_DIM + ROPE_DIM
N_EXPERTS = 8
DIL_CONFIGS = ((128, 1), (512, 4), (2048, 16))
DIL_PAD = max(w for w, _ in DIL_CONFIGS)
DIL_TQ = 512
LATENT_COLS = 1536
PROJ_TN = 512
ROPE_THETA = 10000.0
EPS = 1e-6
NEG_INF = -1e30
LOG2_E = 1.4426950408889634
VMEM_LIMIT = 56 * 1024 * 1024


def _cparams(sem):
    return pltpu.CompilerParams(dimension_semantics=sem, vmem_limit_bytes=VMEM_LIMIT)


def _rms(xf, g):
    ms = jnp.mean(xf * xf, axis=-1, keepdims=True)
    return xf * lax.rsqrt(ms + EPS) * g


def _rope_full(a, c, s):
    return a * c + pltpu.roll(a, HEAD_DIM // 2, 1) * s


def _rope_half(a, c, s):
    lane = lax.broadcasted_iota(jnp.int32, a.shape, 1)
    lower = (lane % ROPE_DIM) < (ROPE_DIM // 2)
    partner = jnp.where(lower, pltpu.roll(a, LANES - ROPE_DIM // 2, 1),
                        pltpu.roll(a, ROPE_DIM // 2, 1))
    return a * c + partner * s


def _norm_mm_kernel(x_ref, g_ref, w_ref, cos_ref, sin_ref, o_ref, h_ref, *,
                    k_norm, rope_tiles, scale_tiles, scale):
    j = pl.program_id(1)

    @pl.when(j == 0)
    def _():
        xf = x_ref[:, :k_norm].astype(F32)
        h_ref[...] = _rms(xf, g_ref[...]).astype(BF16)

    acc = jnp.dot(h_ref[...], w_ref[...], preferred_element_type=F32)
    if rope_tiles is None:
        o_ref[...] = acc.astype(o_ref.dtype)
        return
    rope = (j >= rope_tiles[0]) & (j < rope_tiles[1])

    @pl.when(rope)
    def _():
        scaled = j < scale_tiles
        c = jnp.where(scaled, cos_ref[...] * scale, cos_ref[...])
        s = jnp.where(scaled, sin_ref[...] * scale, sin_ref[...])
        for k in range(acc.shape[1] // HEAD_DIM):
            sl = slice(k * HEAD_DIM, (k + 1) * HEAD_DIM)
            o_ref[:, sl] = _rope_full(acc[:, sl], c, s).astype(o_ref.dtype)

    @pl.when(jnp.logical_not(rope))
    def _():
        o_ref[...] = acc.astype(o_ref.dtype)


def _norm_mm(x, g, w, cos, sin, *, x_block, k_norm, tm, tn, rope_tiles=None, scale_tiles=0,
             scale=1.0, name):
    n = x.shape[0]
    xw, xb = x_block
    kw, nout = w.shape
    assert kw == k_norm and n % tm == 0 and nout % tn == 0
    ns = cos.shape[0] // tm
    kern = functools.partial(_norm_mm_kernel, k_norm=k_norm, rope_tiles=rope_tiles,
                             scale_tiles=scale_tiles, scale=scale)
    return pl.pallas_call(
        kern,
        grid=(n // tm, nout // tn),
        in_specs=[
            pl.BlockSpec((tm, xw), lambda i, j: (i, xb)),
            pl.BlockSpec((1, k_norm), lambda i, j: (0, 0)),
            pl.BlockSpec((k_norm, tn), lambda i, j: (0, j)),
            pl.BlockSpec((tm, HEAD_DIM), lambda i, j: (i % ns, 0)),
            pl.BlockSpec((tm, HEAD_DIM), lambda i, j: (i % ns, 0)),
        ],
        out_specs=pl.BlockSpec((tm, tn), lambda i, j: (i, j)),
        out_shape=jax.ShapeDtypeStruct((n, nout), BF16),
        scratch_shapes=[pltpu.VMEM((tm, k_norm), BF16)],
        compiler_params=_cparams(("arbitrary", "arbitrary")),
        name=name,
    )(x, g.reshape(1, -1), w, cos, sin)


def _q_up_kernel(x_ref, g_ref, w_ref, cos_ref, sin_ref, q_ref, kr_ref, h_ref, *, scale):
    j = pl.program_id(1)
    c = cos_ref[...]
    s = sin_ref[...]

    @pl.when(j == 0)
    def _():
        blk = x_ref[...].astype(F32)
        h_ref[...] = _rms(blk[:, :Q_LORA], g_ref[...]).astype(BF16)
        kr_ref[...] = _rope_half(blk[:, Q_LORA:Q_LORA + LANES], c, s).astype(BF16)

    acc = jnp.dot(h_ref[...], w_ref[...], preferred_element_type=F32)
    for hh in range(acc.shape[1] // (2 * LANES)):
        lo = hh * 2 * LANES
        q_ref[:, lo:lo + LANES] = (acc[:, lo:lo + LANES] * scale).astype(BF16)
        q_ref[:, lo + LANES:lo + 2 * LANES] = (
            _rope_half(acc[:, lo + LANES:lo + 2 * LANES], c, s) * scale).astype(BF16)


def _q_up(lat, g, w, cos, sin, *, tm, tn, scale):
    n = lat.shape[0]
    nout = w.shape[1]
    ns = cos.shape[0] // tm
    return pl.pallas_call(
        functools.partial(_q_up_kernel, scale=scale),
        grid=(n // tm, nout // tn),
        in_specs=[
            pl.BlockSpec((tm, 1024), lambda i, j: (i, 0)),
            pl.BlockSpec((1, Q_LORA), lambda i, j: (0, 0)),
            pl.BlockSpec((Q_LORA, tn), lambda i, j: (0, j)),
            pl.BlockSpec((tm, LANES), lambda i, j: (i % ns, 0)),
            pl.BlockSpec((tm, LANES), lambda i, j: (i % ns, 0)),
        ],
        out_specs=[
            pl.BlockSpec((tm, tn), lambda i, j: (i, j)),
            pl.BlockSpec((tm, LANES), lambda i, j: (i, 0)),
        ],
        out_shape=[jax.ShapeDtypeStruct((n, nout), BF16),
                   jax.ShapeDtypeStruct((n, LANES), BF16)],
        scratch_shapes=[pltpu.VMEM((tm, Q_LORA), BF16)],
        compiler_params=_cparams(("arbitrary", "arbitrary")),
        name="q_up",
    )(lat, g.reshape(1, -1), w, cos, sin)


def _mla_kernel(q_ref, kn_ref, kr_ref, v_ref, o_ref, kcat_ref, vt_ref, qt_ref, m_ref,
                acc_ref, s_ref, *, tq, tk):
    qi = pl.program_id(2)
    seq = kcat_ref.shape[0]
    vt_rows = vt_ref.shape[1]

    @pl.when(qi == 0)
    def _():
        kcat_ref[:, :HEAD_DIM] = kn_ref[...]
        kcat_ref[:, HEAD_DIM:] = kr_ref[...]
        ones_row = (lax.broadcasted_iota(jnp.int32, (vt_rows - HEAD_DIM, tk), 0) == 0)

        def transpose_chunk(c, carry):
            off = pl.multiple_of(c * tk, tk)
            vt_ref[c, :HEAD_DIM, :] = v_ref[pl.ds(off, tk), :].astype(F32).T.astype(BF16)
            vt_ref[c, HEAD_DIM:, :] = ones_row.astype(BF16)
            return carry

        lax.fori_loop(0, seq // tk, transpose_chunk, 0)

    qt_ref[...] = q_ref[...].astype(F32).T.astype(BF16)
    m_ref[...] = jnp.full(m_ref.shape, NEG_INF, F32)
    acc_ref[...] = jnp.zeros(acc_ref.shape, F32)
    per_q = tq // tk

    def scores(j):
        off = pl.multiple_of(j * tk, tk)
        return jnp.dot(kcat_ref[pl.ds(off, tk), :], qt_ref[...],
                       preferred_element_type=F32)

    def chunk(j, masked, has_next):
        s = s_ref[j % 2]
        if has_next:
            s_ref[(j + 1) % 2] = scores(j + 1)
        if masked:
            key = j * tk + lax.broadcasted_iota(jnp.int32, (tk, tq), 0)
            qry = qi * tq + lax.broadcasted_iota(jnp.int32, (tk, tq), 1)
            s = jnp.where(key <= qry, s, NEG_INF)
        m_old = m_ref[...]
        m_new = jnp.maximum(m_old, jnp.max(s, axis=0, keepdims=True))
        a = jnp.exp2(m_old - m_new)
        p = jnp.exp2(s - m_new).astype(BF16)
        acc_ref[...] = a * acc_ref[...] + jnp.dot(vt_ref[j], p, preferred_element_type=F32)
        m_ref[...] = m_new

    def body(j, carry):
        chunk(j, False, True)
        return carry

    n_full = qi * per_q
    s_ref[0] = scores(0)
    lax.fori_loop(0, n_full, body, 0)
    for d in range(per_q):
        chunk(n_full + d, True, d + 1 < per_q)
    out_t = acc_ref[:HEAD_DIM, :] / acc_ref[HEAD_DIM:HEAD_DIM + 1, :]
    o_ref[...] = out_t.T.astype(o_ref.dtype)


def _mla_attention(q, kv, kr, *, batch, seq, tq, tk):
    n = q.shape[0]
    nq = seq // tq
    return pl.pallas_call(
        functools.partial(_mla_kernel, tq=tq, tk=tk),
        grid=(batch, MLA_HEADS, nq),
        in_specs=[
            pl.BlockSpec((tq, 2 * LANES), lambda b, h, i: (b * nq + i, h)),
            pl.BlockSpec((seq, HEAD_DIM), lambda b, h, i: (b, h)),
            pl.BlockSpec((seq, LANES), lambda b, h, i: (b, 0)),
            pl.BlockSpec((seq, HEAD_DIM), lambda b, h, i: (b, MLA_HEADS + h)),
        ],
        out_specs=pl.BlockSpec((tq, HEAD_DIM), lambda b, h, i: (b * nq + i, h)),
        out_shape=jax.ShapeDtypeStruct((n, MLA_HEADS * HEAD_DIM), BF16),
        scratch_shapes=[
            pltpu.VMEM((seq, 2 * LANES), BF16),
            pltpu.VMEM((seq // tk, HEAD_DIM + BF16_SUBLANES, tk), BF16),
            pltpu.VMEM((2 * LANES, tq), BF16),
            pltpu.VMEM((1, tq), F32),
            pltpu.VMEM((HEAD_DIM + BF16_SUBLANES, tq), F32),
            pltpu.VMEM((2, tk, tq), F32),
        ],
        compiler_params=_cparams(("arbitrary", "arbitrary", "arbitrary")),
        name="mla_attention",
    )(q, kv, kr, kv)


def _dil_bias(tq):
    kk = jnp.arange(DIL_PAD + tq, dtype=jnp.int32)[:, None]
    qq = jnp.arange(tq, dtype=jnp.int32)[None, :]
    delta = qq + DIL_PAD - kk
    count = jnp.zeros(delta.shape, F32)
    for win, dil in DIL_CONFIGS:
        count += ((delta >= 0) & (delta <= win) & (delta % dil == 0)).astype(F32)
    return jnp.where(count > 0, jnp.log2(jnp.maximum(count, 1.0)), NEG_INF)


def _dil_kernel(q_ref, k_ref, v_ref, bias_ref, o_ref, kpad_ref, vt_ref, s_ref, *, tq):
    i = pl.program_id(2)
    seq = k_ref.shape[0]
    ck = vt_ref.shape[2]
    vt_rows = vt_ref.shape[1]
    n_pad = DIL_PAD // ck
    win = DIL_PAD + tq

    @pl.when(i == 0)
    def _():
        kpad_ref[:DIL_PAD, :] = jnp.zeros((DIL_PAD, HEAD_DIM), BF16)
        kpad_ref[DIL_PAD:, :] = k_ref[...]
        ones_row = (lax.broadcasted_iota(jnp.int32, (vt_rows - HEAD_DIM, ck), 0) == 0)
        for c in range(n_pad):
            vt_ref[c] = jnp.zeros((vt_rows, ck), BF16)

        def transpose_chunk(c, carry):
            off = pl.multiple_of(c * ck, ck)
            vt_ref[n_pad + c, :HEAD_DIM, :] = v_ref[pl.ds(off, ck), :].astype(F32).T.astype(BF16)
            vt_ref[n_pad + c, HEAD_DIM:, :] = ones_row.astype(BF16)
            return carry

        lax.fori_loop(0, seq // ck, transpose_chunk, 0)

    s0 = pl.multiple_of(i * tq, tq)
    qt = q_ref[...].astype(F32).T.astype(BF16)
    s_ref[...] = jnp.dot(kpad_ref[pl.ds(s0, win), :], qt,
                         preferred_element_type=F32) + bias_ref[...]

    @pl.when(s0 < DIL_PAD)
    def _():
        row = lax.broadcasted_iota(jnp.int32, (win, tq), 0)
        s_ref[...] = jnp.where(row >= DIL_PAD - s0, s_ref[...], NEG_INF)

    s = s_ref[...]
    m = jnp.max(s, axis=0, keepdims=True)
    p = jnp.exp2(s - m).astype(BF16)
    base = i * (tq // ck)
    acc = jnp.zeros((vt_rows, tq), F32)
    for c in range(win // ck):
        acc += jnp.dot(vt_ref[base + c], p[c * ck:(c + 1) * ck, :], preferred_element_type=F32)
    o_ref[...] = (acc[:HEAD_DIM, :] / acc[HEAD_DIM:HEAD_DIM + 1, :]).T


def _dilated(qkv, bias, *, batch, seq, tq, ck):
    n = qkv.shape[0]
    nq = seq // tq
    win = DIL_PAD + tq
    vt_rows = HEAD_DIM + BF16_SUBLANES
    q0 = LATENT_COLS // HEAD_DIM
    return pl.pallas_call(
        functools.partial(_dil_kernel, tq=tq),
        grid=(batch, DIL_HEADS, nq),
        in_specs=[
            pl.BlockSpec((tq, HEAD_DIM), lambda b, h, i: (b * nq + i, q0 + h)),
            pl.BlockSpec((seq, HEAD_DIM), lambda b, h, i: (b, q0 + DIL_HEADS + h)),
            pl.BlockSpec((seq, HEAD_DIM), lambda b, h, i: (b, q0 + 2 * DIL_HEADS + h)),
            pl.BlockSpec((win, tq), lambda b, h, i: (0, 0)),
        ],
        out_specs=pl.BlockSpec((tq, HEAD_DIM), lambda b, h, i: (b * nq + i, h)),
        out_shape=jax.ShapeDtypeStruct((n, DIL_HEADS * HEAD_DIM), F32),
        scratch_shapes=[
            pltpu.VMEM((DIL_PAD + seq, HEAD_DIM), BF16),
            pltpu.VMEM(((DIL_PAD + seq) // ck, vt_rows, ck), BF16),
            pltpu.VMEM((win, tq), F32),
        ],
        compiler_params=_cparams(("arbitrary", "arbitrary", "arbitrary")),
        name="dilated_attention",
    )(qkv, qkv, qkv, bias)


def _out_proj_kernel(oa_ref, ob_ref, ga_ref, gb_ref, w_ref, x_ref, o_ref, h_ref):
    half = oa_ref.shape[1]

    @pl.when(pl.program_id(1) == 0)
    def _():
        h_ref[:, :half] = _rms(oa_ref[...].astype(F32), ga_ref[...]).astype(BF16)
        h_ref[:, half:] = _rms(ob_ref[...], gb_ref[...]).astype(BF16)

    o_ref[...] = x_ref[...] + jnp.dot(h_ref[...], w_ref[...], preferred_element_type=F32)


def _out_proj(o_a, o_b, g_a, g_b, w, x, *, tm, tn):
    n, d = x.shape
    half = o_a.shape[1]
    row = lambda i, j: (i, 0)
    return pl.pallas_call(
        _out_proj_kernel,
        grid=(n // tm, d // tn),
        in_specs=[pl.BlockSpec((tm, half), row)] * 2
        + [pl.BlockSpec((1, half), lambda i, j: (0, 0))] * 2
        + [pl.BlockSpec((2 * half, tn), lambda i, j: (0, j)),
           pl.BlockSpec((tm, tn), lambda i, j: (i, j))],
        out_specs=pl.BlockSpec((tm, tn), lambda i, j: (i, j)),
        out_shape=jax.ShapeDtypeStruct((n, d), F32),
        scratch_shapes=[pltpu.VMEM((tm, 2 * half), BF16)],
        compiler_params=_cparams(("arbitrary", "arbitrary")),
        name="out_proj",
    )(o_a, o_b, g_a.reshape(1, -1), g_b.reshape(1, -1), w, x)


def _gate_up_kernel(x_ref, g_ref, wg_ref, wu_ref, o_ref, h_ref):
    @pl.when(pl.program_id(1) == 0)
    def _():
        h_ref[...] = _rms(x_ref[...], g_ref[...]).astype(BF16)

    h = h_ref[...]
    gate = jnp.dot(h, wg_ref[...], preferred_element_type=F32)
    up = jnp.dot(h, wu_ref[...], preferred_element_type=F32)
    o_ref[...] = (gate * jax.nn.sigmoid(gate) * up).astype(o_ref.dtype)


def _gate_up(x, g, wg, wu, *, tm, tn):
    n, d = x.shape
    f = wg.shape[1]
    return pl.pallas_call(
        _gate_up_kernel,
        grid=(n // tm, f // tn),
        in_specs=[
            pl.BlockSpec((tm, d), lambda i, j: (i, 0)),
            pl.BlockSpec((1, d), lambda i, j: (0, 0)),
            pl.BlockSpec((d, tn), lambda i, j: (0, j)),
            pl.BlockSpec((d, tn), lambda i, j: (0, j)),
        ],
        out_specs=pl.BlockSpec((tm, tn), lambda i, j: (i, j)),
        out_shape=jax.ShapeDtypeStruct((n, f), BF16),
        scratch_shapes=[pltpu.VMEM((tm, d), BF16)],
        compiler_params=_cparams(("arbitrary", "arbitrary")),
        name="ffn_gate_up",
    )(x, g.reshape(1, -1), wg, wu)


def _down_kernel(a_ref, w_ref, x_ref, o_ref):
    o_ref[...] = x_ref[...] + jnp.dot(a_ref[...], w_ref[...], preferred_element_type=F32)


def _down(act, w, x, *, tm, tn):
    n, f = act.shape
    d = w.shape[1]
    return pl.pallas_call(
        _down_kernel,
        grid=(n // tm, d // tn),
        in_specs=[
            pl.BlockSpec((tm, f), lambda i, j: (i, 0)),
            pl.BlockSpec((f, tn), lambda i, j: (0, j)),
            pl.BlockSpec((tm, tn), lambda i, j: (i, j)),
        ],
        out_specs=pl.BlockSpec((tm, tn), lambda i, j: (i, j)),
        out_shape=jax.ShapeDtypeStruct((n, d), F32),
        compiler_params=_cparams(("arbitrary", "arbitrary")),
        name="ffn_down",
    )(act, w, x)


def _router_kernel(x_ref, g_ref, w_ref, idx_ref, wt_ref):
    h = _rms(x_ref[...], g_ref[...])
    logits = jnp.dot(h, w_ref[...], preferred_element_type=F32,
                     precision=lax.Precision.HIGHEST)
    lane_i = lax.broadcasted_iota(jnp.int32, logits.shape, 1)
    lane = lane_i.astype(F32)
    logits = jnp.where(lane_i < N_EXPERTS, logits, -jnp.inf)
    v1 = jnp.max(logits, axis=-1, keepdims=True)
    i1 = jnp.min(jnp.where(logits == v1, lane, float(LANES)), axis=-1, keepdims=True)
    rest = jnp.where(lane == i1, -jnp.inf, logits)
    v2 = jnp.max(rest, axis=-1, keepdims=True)
    i2 = jnp.min(jnp.where(rest == v2, lane, float(LANES)), axis=-1, keepdims=True)
    e2 = jnp.exp(v2 - v1)
    w1 = 1.0 / (1.0 + e2)
    w2 = e2 / (1.0 + e2)
    idx_ref[...] = jnp.where(lane_i == 0, i1, jnp.where(lane_i == 1, i2, 0.0)).astype(jnp.int32)
    wt_ref[...] = jnp.where(lane_i == 0, w1, jnp.where(lane_i == 1, w2, 0.0))


def _router(x, g, w_pad, *, tm):
    n, d = x.shape
    return pl.pallas_call(
        _router_kernel,
        grid=(n // tm,),
        in_specs=[
            pl.BlockSpec((tm, d), lambda i: (i, 0)),
            pl.BlockSpec((1, d), lambda i: (0, 0)),
            pl.BlockSpec((d, LANES), lambda i: (0, 0)),
        ],
        out_specs=[pl.BlockSpec((tm, LANES), lambda i: (i, 0))] * 2,
        out_shape=[jax.ShapeDtypeStruct((n, LANES), jnp.int32),
                   jax.ShapeDtypeStruct((n, LANES), F32)],
        compiler_params=_cparams(("arbitrary",)),
        name="moe_router",
    )(x, g.reshape(1, -1), w_pad)


def _row_copy(src_hbm, row, dst_vmem, slot, sem):
    return pltpu.make_async_copy(src_hbm.at[pl.ds(row, 1)], dst_vmem.at[pl.ds(slot, 1)], sem)


DMA_UNROLL = 8
DMA_PRIORITIES = 2


def _load_indices(idx_hbm, tile, idx_smem, isem):
    cp = pltpu.make_async_copy(idx_hbm.at[tile], idx_smem, isem)
    cp.start()
    cp.wait()


def _start_rows(idx_smem, src_hbm, dst_vmem, sem):
    def start(i, c):
        for prio in range(DMA_PRIORITIES):
            row = DMA_PRIORITIES * i + prio
            _row_copy(src_hbm, idx_smem[row], dst_vmem, row, sem).start(priority=prio)
        return c

    lax.fori_loop(0, dst_vmem.shape[0] // DMA_PRIORITIES, start, 0,
                  unroll=DMA_UNROLL // DMA_PRIORITIES)


def _wait_rows(src_hbm, dst_vmem, sem):
    def wait(i, c):
        _row_copy(src_hbm, 0, dst_vmem, i, sem).wait()
        return c

    lax.fori_loop(0, dst_vmem.shape[0], wait, 0, unroll=DMA_UNROLL)


def _gather_norm_kernel(nu_ref, idx_hbm, x_hbm, g_ref, o_ref, idx_smem, buf, isem, sem):
    i = pl.program_id(0)
    n_used = nu_ref[0]

    def fetch(tile, slot):
        _load_indices(idx_hbm, tile, idx_smem, isem)
        _start_rows(idx_smem, x_hbm, buf.at[slot], sem.at[slot])

    @pl.when((i == 0) & (n_used > 0))
    def _():
        fetch(0, 0)

    @pl.when(i + 1 < n_used)
    def _():
        fetch(i + 1, (i + 1) % 2)

    @pl.when(i < n_used)
    def _():
        slot = i % 2
        _wait_rows(x_hbm, buf.at[slot], sem.at[slot])
        o_ref[...] = _rms(buf[slot], g_ref[...]).astype(o_ref.dtype)

    @pl.when(i >= n_used)
    def _():
        o_ref[...] = jnp.zeros(o_ref.shape, o_ref.dtype)


def _gather_norm(n_used, row_token, x, g, *, tr):
    nt = row_token.shape[0]
    d = x.shape[1]
    grid_spec = pltpu.PrefetchScalarGridSpec(
        num_scalar_prefetch=1,
        grid=(nt,),
        in_specs=[
            pl.BlockSpec(memory_space=pl.ANY),
            pl.BlockSpec(memory_space=pl.ANY),
            pl.BlockSpec((1, d), lambda i, nu: (0, 0)),
        ],
        out_specs=pl.BlockSpec((tr, d), lambda i, nu: (i, 0)),
        scratch_shapes=[
            pltpu.SMEM((tr,), jnp.int32),
            pltpu.VMEM((2, tr, d), F32),
            pltpu.SemaphoreType.DMA,
            pltpu.SemaphoreType.DMA((2,)),
        ],
    )
    return pl.pallas_call(
        _gather_norm_kernel,
        grid_spec=grid_spec,
        out_shape=jax.ShapeDtypeStruct((nt * tr, d), BF16),
        compiler_params=_cparams(("arbitrary",)),
        name="moe_gather",
    )(n_used, row_token, x, g.reshape(1, -1))


def _moe_gate_up_kernel(te_ref, nu_ref, nxt_ref, x_ref, wg_hbm, wu_hbm, o_ref, stage_g, stage_u,
                        wg_bf, wu_bf, sem, *, layer, tn):
    j = pl.program_id(0)
    i = pl.program_id(1)
    n_col = pl.num_programs(0)
    used = i < nu_ref[0]
    e = te_ref[i]
    fresh = (i == 0) | (e != te_ref[jnp.maximum(i - 1, 0)])

    def window_copies(col, expert):
        cols = pl.ds(pl.multiple_of(col * tn, tn), tn)
        return (pltpu.make_async_copy(wg_hbm.at[layer, expert, :, cols], stage_g, sem.at[0]),
                pltpu.make_async_copy(wu_hbm.at[layer, expert, :, cols], stage_u, sem.at[1]))

    def start_window(col, expert):
        for cp in window_copies(col, expert):
            cp.start()

    @pl.when((j == 0) & (i == 0) & used)
    def _():
        start_window(0, e)

    @pl.when(used & fresh)
    def _():
        for cp in window_copies(j, e):
            cp.wait()
        wg_bf[...] = stage_g[...].astype(BF16)
        wu_bf[...] = stage_u[...].astype(BF16)
        nxt = nxt_ref[e]

        @pl.when(nxt >= 0)
        def _():
            start_window(j, nxt)

        @pl.when((nxt < 0) & (j + 1 < n_col))
        def _():
            start_window(j + 1, te_ref[0])

    @pl.when(used)
    def _():
        h = x_ref[...]
        gate = jnp.dot(h, wg_bf[...], preferred_element_type=F32)
        up = jnp.dot(h, wu_bf[...], preferred_element_type=F32)
        o_ref[...] = (gate * jax.nn.sigmoid(gate) * up).astype(o_ref.dtype)

    @pl.when(jnp.logical_not(used))
    def _():
        o_ref[...] = jnp.zeros(o_ref.shape, o_ref.dtype)


def _moe_gate_up(tile_expert, n_used, next_expert, xs, wg, wu, layer, *, tm, tn):
    r, d = xs.shape
    f = wg.shape[3]
    grid_spec = pltpu.PrefetchScalarGridSpec(
        num_scalar_prefetch=3,
        grid=(f // tn, r // tm),
        in_specs=[pl.BlockSpec((tm, d), lambda j, i, te, nu, nx: (i, 0)),
                  pl.BlockSpec(memory_space=pl.ANY),
                  pl.BlockSpec(memory_space=pl.ANY)],
        out_specs=pl.BlockSpec((tm, tn), lambda j, i, te, nu, nx: (i, j)),
        scratch_shapes=[pltpu.VMEM((d, tn), F32), pltpu.VMEM((d, tn), F32),
                        pltpu.VMEM((d, tn), BF16), pltpu.VMEM((d, tn), BF16),
                        pltpu.SemaphoreType.DMA((2,))],
    )
    return pl.pallas_call(
        functools.partial(_moe_gate_up_kernel, layer=layer, tn=tn),
        grid_spec=grid_spec,
        out_shape=jax.ShapeDtypeStruct((r, f), BF16),
        compiler_params=_cparams(("arbitrary", "arbitrary")),
        name="moe_gate_up",
    )(tile_expert, n_used, next_expert, xs, wg, wu)


def _moe_down_kernel(te_ref, nu_ref, a_ref, w_ref, o_ref):
    i = pl.program_id(1)

    @pl.when(i < nu_ref[0])
    def _():
        o_ref[...] = jnp.dot(a_ref[...], w_ref[0], preferred_element_type=F32)

    @pl.when(i >= nu_ref[0])
    def _():
        o_ref[...] = jnp.zeros(o_ref.shape, o_ref.dtype)


def _moe_down(tile_expert, n_used, act, w, layer, *, tm, tn):
    r, f = act.shape
    d = w.shape[3]
    grid_spec = pltpu.PrefetchScalarGridSpec(
        num_scalar_prefetch=2,
        grid=(d // tn, r // tm),
        in_specs=[
            pl.BlockSpec((tm, f), lambda j, i, te, nu: (i, 0)),
            pl.BlockSpec((None, 1, f, tn), lambda j, i, te, nu: (layer, te[i], 0, j)),
        ],
        out_specs=pl.BlockSpec((tm, tn), lambda j, i, te, nu: (i, j)),
    )
    return pl.pallas_call(
        _moe_down_kernel,
        grid_spec=grid_spec,
        out_shape=jax.ShapeDtypeStruct((r, d), F32),
        compiler_params=_cparams(("arbitrary", "arbitrary")),
        name="moe_down",
    )(tile_expert, n_used, act, w)


def _combine_kernel(p0_hbm, p1_hbm, ys_hbm, x_ref, wt_ref, g_ref, o_ref, i0_smem, i1_smem,
                    b0, b1, isem, sem0, sem1, *, final_norm):
    i = pl.program_id(0)
    n = pl.num_programs(0)

    def fetch(tile, slot):
        _load_indices(p0_hbm, tile, i0_smem, isem)
        _load_indices(p1_hbm, tile, i1_smem, isem)
        _start_rows(i0_smem, ys_hbm, b0.at[slot], sem0.at[slot])
        _start_rows(i1_smem, ys_hbm, b1.at[slot], sem1.at[slot])

    @pl.when(i == 0)
    def _():
        fetch(0, 0)

    @pl.when(i + 1 < n)
    def _():
        fetch(i + 1, (i + 1) % 2)

    slot = i % 2
    _wait_rows(ys_hbm, b0.at[slot], sem0.at[slot])
    _wait_rows(ys_hbm, b1.at[slot], sem1.at[slot])
    wt = wt_ref[...]
    y = x_ref[...] + (wt[:, 0:1] * b0[slot] + wt[:, 1:2] * b1[slot])
    if final_norm:
        y = _rms(y, g_ref[...])
    o_ref[...] = y


def _combine(pos0, pos1, ys, x, wt, g, *, tr, final_norm):
    n, d = x.shape
    return pl.pallas_call(
        functools.partial(_combine_kernel, final_norm=final_norm),
        grid=(n // tr,),
        in_specs=[
            pl.BlockSpec(memory_space=pl.ANY),
            pl.BlockSpec(memory_space=pl.ANY),
            pl.BlockSpec(memory_space=pl.ANY),
            pl.BlockSpec((tr, d), lambda i: (i, 0)),
            pl.BlockSpec((tr, LANES), lambda i: (i, 0)),
            pl.BlockSpec((1, d), lambda i: (0, 0)),
        ],
        out_specs=pl.BlockSpec((tr, d), lambda i: (i, 0)),
        out_shape=jax.ShapeDtypeStruct((n, d), F32),
        scratch_shapes=[
            pltpu.SMEM((tr,), jnp.int32),
            pltpu.SMEM((tr,), jnp.int32),
            pltpu.VMEM((2, tr, d), F32),
            pltpu.VMEM((2, tr, d), F32),
            pltpu.SemaphoreType.DMA,
            pltpu.SemaphoreType.DMA((2,)),
            pltpu.SemaphoreType.DMA((2,)),
        ],
        compiler_params=_cparams(("arbitrary",)),
        name="moe_combine",
    )(pos0.reshape(n // tr, tr), pos1.reshape(n // tr, tr), ys, x, wt, g.reshape(1, -1))


def _rmsnorm_kernel(x_ref, g_ref, o_ref):
    o_ref[...] = _rms(x_ref[...], g_ref[...])


def _final_norm(x, g, *, tm):
    n, d = x.shape
    return pl.pallas_call(
        _rmsnorm_kernel,
        grid=(n // tm,),
        in_specs=[pl.BlockSpec((tm, d), lambda i: (i, 0)),
                  pl.BlockSpec((1, d), lambda i: (0, 0))],
        out_specs=pl.BlockSpec((tm, d), lambda i: (i, 0)),
        out_shape=jax.ShapeDtypeStruct((n, d), F32),
        compiler_params=_cparams(("arbitrary",)),
        name="final_norm",
    )(x, g.reshape(1, -1))


def _routing_plan(top_idx, *, tm):
    n = top_idx.shape[0]
    n_assign = 2 * n
    rows = n_assign + N_EXPERTS * tm
    n_tiles = rows // tm
    e_flat = top_idx.reshape(n_assign)
    onehot = (e_flat[:, None] == jnp.arange(N_EXPERTS, dtype=jnp.int32)[None, :]).astype(jnp.int32)
    counts = jnp.sum(onehot, axis=0)
    rank = jnp.sum((jnp.cumsum(onehot, axis=0) - onehot) * onehot, axis=1)
    padded = ((counts + tm - 1) // tm) * tm
    ends = jnp.cumsum(padded)
    starts = ends - padded
    dest = starts[e_flat] + rank
    token = jnp.arange(n_assign, dtype=jnp.int32) // 2
    row_token = jnp.zeros((rows,), jnp.int32).at[dest].set(token)
    tile_start = jnp.arange(n_tiles, dtype=jnp.int32) * tm
    tile_expert = jnp.sum((tile_start[:, None] >= ends[None, :]).astype(jnp.int32), axis=1)
    tile_expert = jnp.minimum(tile_expert, N_EXPERTS - 1).astype(jnp.int32)
    n_used = (ends[-1] // tm).astype(jnp.int32).reshape(1)
    pos = dest.reshape(n, 2).astype(jnp.int32)
    experts = jnp.arange(N_EXPERTS, dtype=jnp.int32)
    later = (padded > 0)[None, :] & (experts[None, :] > experts[:, None])
    nxt = jnp.min(jnp.where(later, experts[None, :], N_EXPERTS), axis=1)
    next_expert = jnp.where(nxt == N_EXPERTS, -1, nxt).astype(jnp.int32)
    return row_token.reshape(n_tiles, tm), tile_expert, n_used, next_expert, pos


def _moe_layer(x, g, router_pad, wg, wu, wd, layer, final_g, *, final_norm):
    tm = 512
    idx, wt = _router(x, g, router_pad, tm=1024)
    row_token, tile_expert, n_used, next_expert, pos = _routing_plan(idx[:, :2], tm=tm)
    xs = _gather_norm(n_used, row_token, x, g, tr=tm)
    act = _moe_gate_up(tile_expert, n_used, next_expert, xs, wg, wu, layer, tm=tm, tn=1024)
    ys = _moe_down(tile_expert, n_used, act, wd, layer, tm=tm, tn=1024)
    return _combine(pos[:, 0], pos[:, 1], ys, x, wt, final_g, tr=256, final_norm=final_norm)


def _rope_tables(seq):
    pos = jnp.arange(seq, dtype=F32)[:, None]

    def tab(dim):
        inv = ROPE_THETA ** (-jnp.arange(0, dim, 2, dtype=F32) / dim)
        ang = pos * inv[None, :]
        return jnp.cos(ang), jnp.sin(ang)

    cf, sf = tab(HEAD_DIM)
    cos_f = jnp.concatenate([cf, cf], axis=1)
    sin_f = jnp.concatenate([-sf, sf], axis=1)
    cr, sr = tab(ROPE_DIM)
    zero = jnp.zeros((seq, LANES - ROPE_DIM), F32)
    cos_r = jnp.concatenate([cr, cr, zero], axis=1)
    sin_r = jnp.concatenate([-sr, sr, zero], axis=1)
    return cos_f, sin_f, cos_r, sin_r


def kernel(x, attn_norm_g, w_in, q_norm_g, kv_norm_g, w_uq, w_uk, w_uv, out_a_norm_g,
           out_b_norm_g, w_out, ffn_norm_g, dense_w_gate, dense_w_up, dense_w_down,
           router_w, moe_w_gate, moe_w_up, moe_w_down, final_norm_g):
    batch, seq, d_model = x.shape
    depth = w_in.shape[0]
    n = batch * seq
    mix = DIL_HEADS * HEAD_DIM
    cos_f, sin_f, cos_r, sin_r = _rope_tables(seq)
    dil_bias = _dil_bias(DIL_TQ)

    c0, c1, c2 = Q_LORA, Q_LORA + KV_LORA, Q_LORA + KV_LORA + ROPE_DIM
    lat_tiles = LATENT_COLS // PROJ_TN

    def proj_weight(wi):
        pad = jnp.zeros((d_model, 1024 - Q_LORA - ROPE_DIM), F32)
        return jnp.concatenate([wi[:, :c0], wi[:, c1:c2], pad, wi[:, c0:c1], wi[:, c2:]],
                               axis=1).astype(BF16)

    def q_up_weight(wq):
        uq = wq.reshape(Q_LORA, MLA_HEADS, QK_DIM)
        uq = jnp.concatenate([uq, jnp.zeros((Q_LORA, MLA_HEADS, 2 * LANES - QK_DIM), F32)], axis=2)
        return uq.reshape(Q_LORA, MLA_HEADS * 2 * LANES).astype(BF16)

    def bf16(w, idx):
        return w[idx].astype(BF16)

    moe_down_bf = moe_w_down.astype(BF16)

    xf = x.reshape(n, d_model)
    for layer in range(depth):
        g_attn = attn_norm_g[layer]
        proj = _norm_mm(xf, g_attn, proj_weight(w_in[layer]), cos_f, sin_f, x_block=(d_model, 0),
                        k_norm=d_model, tm=1024, tn=PROJ_TN,
                        rope_tiles=(lat_tiles, lat_tiles + 2 * mix // PROJ_TN),
                        scale_tiles=lat_tiles + mix // PROJ_TN,
                        scale=LOG2_E * HEAD_DIM ** -0.5, name="in_proj")
        q, kr = _q_up(proj, q_norm_g[layer], q_up_weight(w_uq[layer]), cos_r, sin_r, tm=1024,
                      tn=512, scale=LOG2_E * QK_DIM ** -0.5)
        w_ukv = jnp.concatenate([w_uk[layer], w_uv[layer]], axis=1).astype(BF16)
        kv = _norm_mm(proj, kv_norm_g[layer], w_ukv, cos_f, sin_f,
                      x_block=(KV_LORA, 1024 // KV_LORA), k_norm=KV_LORA, tm=1024, tn=1024,
                      name="kv_up")
        o_a = _mla_attention(q, kv, kr, batch=batch, seq=seq, tq=1024, tk=512)
        o_b = _dilated(proj, dil_bias, batch=batch, seq=seq, tq=DIL_TQ, ck=256)
        xf = _out_proj(o_a, o_b, out_a_norm_g[layer], out_b_norm_g[layer],
                       bf16(w_out, layer), xf, tm=1024, tn=512)
        j = layer // 2
        last = layer == depth - 1
        if layer % 2 == 0:
            act = _gate_up(xf, ffn_norm_g[layer], bf16(dense_w_gate, j), bf16(dense_w_up, j),
                           tm=1024, tn=512)
            xf = _down(act, bf16(dense_w_down, j), xf, tm=1024, tn=512)
            if last:
                xf = _final_norm(xf, final_norm_g, tm=1024)
        else:
            router_pad = jnp.concatenate(
                [router_w[j], jnp.zeros((d_model, LANES - N_EXPERTS), F32)], axis=1)
            xf = _moe_layer(xf, ffn_norm_g[layer], router_pad, moe_w_gate, moe_w_up, moe_down_bf,
                            j, final_norm_g, final_norm=last)
    return xf.reshape(batch, seq, d_model)
```

```python
import functools

import jax
import jax.numpy as jnp
from jax import lax
from jax.experimental import pallas as pl
from jax.experimental.pallas import tpu as pltpu

F32 = jnp.float32
BF16 = jnp.bfloat16

LANES = 128
BF16_SUBLANES = 16
MLA_HEADS = 8
DIL_HEADS = 8
HEAD_DIM = 128
Q_LORA = 768
KV_LORA = 512
ROPE_DIM = 64
QK_DIM = HEAD_DIM + ROPE_DIM
N_EXPERTS = 8
DIL_CONFIGS = ((128, 1), (512, 4), (2048, 16))
DIL_PAD = max(w for w, _ in DIL_CONFIGS)
DIL_TQ = 512
LATENT_COLS = 1536
PROJ_TN = 512
ROPE_THETA = 10000.0
EPS = 1e-6
NEG_INF = -1e30
LOG2_E = 1.4426950408889634
VMEM_LIMIT = 56 * 1024 * 1024


def _cparams(sem):
    return pltpu.CompilerParams(dimension_semantics=sem, vmem_limit_bytes=VMEM_LIMIT)


def _rms(xf, g):
    ms = jnp.mean(xf * xf, axis=-1, keepdims=True)
    return xf * lax.rsqrt(ms + EPS) * g


def _rope_full(a, c, s):
    return a * c + pltpu.roll(a, HEAD_DIM // 2, 1) * s


def _rope_half(a, c, s):
    lane = lax.broadcasted_iota(jnp.int32, a.shape, 1)
    lower = (lane % ROPE_DIM) < (ROPE_DIM // 2)
    partner = jnp.where(lower, pltpu.roll(a, LANES - ROPE_DIM // 2, 1),
                        pltpu.roll(a, ROPE_DIM // 2, 1))
    return a * c + partner * s


def _norm_mm_kernel(x_ref, g_ref, w_ref, cos_ref, sin_ref, o_ref, h_ref, *,
                    k_norm, rope_tiles, scale_tiles, scale):
    j = pl.program_id(1)

    @pl.when(j == 0)
    def _():
        xf = x_ref[:, :k_norm].astype(F32)
        h_ref[...] = _rms(xf, g_ref[...]).astype(BF16)

    acc = jnp.dot(h_ref[...], w_ref[...], preferred_element_type=F32)
    if rope_tiles is None:
        o_ref[...] = acc.astype(o_ref.dtype)
        return
    rope = (j >= rope_tiles[0]) & (j < rope_tiles[1])

    @pl.when(rope)
    def _():
        scaled = j < scale_tiles
        c = jnp.where(scaled, cos_ref[...] * scale, cos_ref[...])
        s = jnp.where(scaled, sin_ref[...] * scale, sin_ref[...])
        for k in range(acc.shape[1] // HEAD_DIM):
            sl = slice(k * HEAD_DIM, (k + 1) * HEAD_DIM)
            o_ref[:, sl] = _rope_full(acc[:, sl], c, s).astype(o_ref.dtype)

    @pl.when(jnp.logical_not(rope))
    def _():
        o_ref[...] = acc.astype(o_ref.dtype)


def _norm_mm(x, g, w, cos, sin, *, x_block, k_norm, tm, tn, rope_tiles=None, scale_tiles=0,
             scale=1.0, name):
    n = x.shape[0]
    xw, xb = x_block
    kw, nout = w.shape
    assert kw == k_norm and n % tm == 0 and nout % tn == 0
    ns = cos.shape[0] // tm
    kern = functools.partial(_norm_mm_kernel, k_norm=k_norm, rope_tiles=rope_tiles,
                             scale_tiles=scale_tiles, scale=scale)
    return pl.pallas_call(
        kern,
        grid=(n // tm, nout // tn),
        in_specs=[
            pl.BlockSpec((tm, xw), lambda i, j: (i, xb)),
            pl.BlockSpec((1, k_norm), lambda i, j: (0, 0)),
            pl.BlockSpec((k_norm, tn), lambda i, j: (0, j)),
            pl.BlockSpec((tm, HEAD_DIM), lambda i, j: (i % ns, 0)),
            pl.BlockSpec((tm, HEAD_DIM), lambda i, j: (i % ns, 0)),
        ],
        out_specs=pl.BlockSpec((tm, tn), lambda i, j: (i, j)),
        out_shape=jax.ShapeDtypeStruct((n, nout), BF16),
        scratch_shapes=[pltpu.VMEM((tm, k_norm), BF16)],
        compiler_params=_cparams(("arbitrary", "arbitrary")),
        name=name,
    )(x, g.reshape(1, -1), w, cos, sin)


def _q_up_kernel(x_ref, g_ref, w_ref, cos_ref, sin_ref, q_ref, kr_ref, h_ref, *, scale):
    j = pl.program_id(1)
    c = cos_ref[...]
    s = sin_ref[...]

    @pl.when(j == 0)
    def _():
        blk = x_ref[...].astype(F32)
        h_ref[...] = _rms(blk[:, :Q_LORA], g_ref[...]).astype(BF16)
        kr_ref[...] = _rope_half(blk[:, Q_LORA:Q_LORA + LANES], c, s).astype(BF16)

    acc = jnp.dot(h_ref[...], w_ref[...], preferred_element_type=F32)
    for hh in range(acc.shape[1] // (2 * LANES)):
        lo = hh * 2 * LANES
        q_ref[:, lo:lo + LANES] = (acc[:, lo:lo + LANES] * scale).astype(BF16)
        q_ref[:, lo + LANES:lo + 2 * LANES] = (
            _rope_half(acc[:, lo + LANES:lo + 2 * LANES], c, s) * scale).astype(BF16)


def _q_up(lat, g, w, cos, sin, *, tm, tn, scale):
    n = lat.shape[0]
    nout = w.shape[1]
    ns = cos.shape[0] // tm
    return pl.pallas_call(
        functools.partial(_q_up_kernel, scale=scale),
        grid=(n // tm, nout // tn),
        in_specs=[
            pl.BlockSpec((tm, 1024), lambda i, j: (i, 0)),
            pl.BlockSpec((1, Q_LORA), lambda i, j: (0, 0)),
            pl.BlockSpec((Q_LORA, tn), lambda i, j: (0, j)),
            pl.BlockSpec((tm, LANES), lambda i, j: (i % ns, 0)),
            pl.BlockSpec((tm, LANES), lambda i, j: (i % ns, 0)),
        ],
        out_specs=[
            pl.BlockSpec((tm, tn), lambda i, j: (i, j)),
            pl.BlockSpec((tm, LANES), lambda i, j: (i, 0)),
        ],
        out_shape=[jax.ShapeDtypeStruct((n, nout), BF16),
                   jax.ShapeDtypeStruct((n, LANES), BF16)],
        scratch_shapes=[pltpu.VMEM((tm, Q_LORA), BF16)],
        compiler_params=_cparams(("arbitrary", "arbitrary")),
        name="q_up",
    )(lat, g.reshape(1, -1), w, cos, sin)


def _mla_kernel(q_ref, kn_ref, kr_ref, v_ref, o_ref, kcat_ref, vt_ref, qt_ref, m_ref,
                acc_ref, s_ref, *, tq, tk):
    qi = pl.program_id(2)
    seq = kcat_ref.shape[0]
    vt_rows = vt_ref.shape[1]

    @pl.when(qi == 0)
    def _():
        kcat_ref[:, :HEAD_DIM] = kn_ref[...]
        kcat_ref[:, HEAD_DIM:] = kr_ref[...]
        ones_row = (lax.broadcasted_iota(jnp.int32, (vt_rows - HEAD_DIM, tk), 0) == 0)

        def transpose_chunk(c, carry):
            off = pl.multiple_of(c * tk, tk)
            vt_ref[c, :HEAD_DIM, :] = v_ref[pl.ds(off, tk), :].astype(F32).T.astype(BF16)
            vt_ref[c, HEAD_DIM:, :] = ones_row.astype(BF16)
            return carry

        lax.fori_loop(0, seq // tk, transpose_chunk, 0)

    qt_ref[...] = q_ref[...].astype(F32).T.astype(BF16)
    m_ref[...] = jnp.full(m_ref.shape, NEG_INF, F32)
    acc_ref[...] = jnp.zeros(acc_ref.shape, F32)
    per_q = tq // tk

    def scores(j, lo):
        off = pl.multiple_of(j * tk, tk)
        return jnp.dot(kcat_ref[pl.ds(off, tk), :], qt_ref[:, lo:],
                       preferred_element_type=F32)

    def chunk(j, masked, lo, next_lo):
        s = s_ref[j % 2, :, lo:]
        if next_lo is not None:
            s_ref[(j + 1) % 2, :, next_lo:] = scores(j + 1, next_lo)
        if masked:
            key = j * tk + lax.broadcasted_iota(jnp.int32, s.shape, 0)
            qry = qi * tq + lo + lax.broadcasted_iota(jnp.int32, s.shape, 1)
            s = jnp.where(key <= qry, s, NEG_INF)
        m_old = m_ref[:, lo:]
        m_new = jnp.maximum(m_old, jnp.max(s, axis=0, keepdims=True))
        a = jnp.exp2(m_old - m_new)
        p = jnp.exp2(s - m_new).astype(BF16)
        acc_ref[:, lo:] = a * acc_ref[:, lo:] + jnp.dot(vt_ref[j], p,
                                                        preferred_element_type=F32)
        m_ref[:, lo:] = m_new

    def body(j, carry):
        chunk(j, False, 0, 0)
        return carry

    n_full = qi * per_q
    s_ref[0] = scores(0, 0)
    lax.fori_loop(0, n_full, body, 0)
    for d in range(per_q):
        chunk(n_full + d, True, d * tk, (d + 1) * tk if d + 1 < per_q else None)
    out_t = acc_ref[:HEAD_DIM, :] / acc_ref[HEAD_DIM:HEAD_DIM + 1, :]
    o_ref[...] = out_t.T.astype(o_ref.dtype)


def _mla_attention(q, kv, kr, *, batch, seq, tq, tk):
    n = q.shape[0]
    nq = seq // tq
    return pl.pallas_call(
        functools.partial(_mla_kernel, tq=tq, tk=tk),
        grid=(batch, MLA_HEADS, nq),
        in_specs=[
            pl.BlockSpec((tq, 2 * LANES), lambda b, h, i: (b * nq + i, h)),
            pl.BlockSpec((seq, HEAD_DIM), lambda b, h, i: (b, h)),
            pl.BlockSpec((seq, LANES), lambda b, h, i: (b, 0)),
            pl.BlockSpec((seq, HEAD_DIM), lambda b, h, i: (b, MLA_HEADS + h)),
        ],
        out_specs=pl.BlockSpec((tq, HEAD_DIM), lambda b, h, i: (b * nq + i, h)),
        out_shape=jax.ShapeDtypeStruct((n, MLA_HEADS * HEAD_DIM), BF16),
        scratch_shapes=[
            pltpu.VMEM((seq, 2 * LANES), BF16),
            pltpu.VMEM((seq // tk, HEAD_DIM + BF16_SUBLANES, tk), BF16),
            pltpu.VMEM((2 * LANES, tq), BF16),
            pltpu.VMEM((1, tq), F32),
            pltpu.VMEM((HEAD_DIM + BF16_SUBLANES, tq), F32),
            pltpu.VMEM((2, tk, tq), F32),
        ],
        compiler_params=_cparams(("arbitrary", "arbitrary", "arbitrary")),
        name="mla_attention",
    )(q, kv, kr, kv)


def _dil_bias(tq):
    kk = jnp.arange(DIL_PAD + tq, dtype=jnp.int32)[:, None]
    qq = jnp.arange(tq, dtype=jnp.int32)[None, :]
    delta = qq + DIL_PAD - kk
    count = jnp.zeros(delta.shape, F32)
    for win, dil in DIL_CONFIGS:
        count += ((delta >= 0) & (delta <= win) & (delta % dil == 0)).astype(F32)
    return jnp.where(count > 0, jnp.log2(jnp.maximum(count, 1.0)), NEG_INF)


def _dil_kernel(q_ref, k_ref, v_ref, bias_ref, o_ref, kpad_ref, vt_ref, s_ref, *, tq):
    i = pl.program_id(2)
    seq = k_ref.shape[0]
    ck = vt_ref.shape[2]
    vt_rows = vt_ref.shape[1]
    n_pad = DIL_PAD // ck
    win = DIL_PAD + tq

    @pl.when(i == 0)
    def _():
        kpad_ref[:DIL_PAD, :] = jnp.zeros((DIL_PAD, HEAD_DIM), BF16)
        kpad_ref[DIL_PAD:, :] = k_ref[...]
        ones_row = (lax.broadcasted_iota(jnp.int32, (vt_rows - HEAD_DIM, ck), 0) == 0)
        for c in range(n_pad):
            vt_ref[c] = jnp.zeros((vt_rows, ck), BF16)

        def transpose_chunk(c, carry):
            off = pl.multiple_of(c * ck, ck)
            vt_ref[n_pad + c, :HEAD_DIM, :] = v_ref[pl.ds(off, ck), :].astype(F32).T.astype(BF16)
            vt_ref[n_pad + c, HEAD_DIM:, :] = ones_row.astype(BF16)
            return carry

        lax.fori_loop(0, seq // ck, transpose_chunk, 0)

    s0 = pl.multiple_of(i * tq, tq)
    qt = q_ref[...].astype(F32).T.astype(BF16)

    def attend(skip):
        rows = win - skip
        s_ref[:rows, :] = jnp.dot(kpad_ref[pl.ds(s0 + skip, rows), :], qt,
                                  preferred_element_type=F32) + bias_ref[skip:, :]
        s = s_ref[:rows, :]
        m = jnp.max(s, axis=0, keepdims=True)
        p = jnp.exp2(s - m).astype(BF16)
        base = i * (tq // ck) + skip // ck
        acc = jnp.zeros((vt_rows, tq), F32)
        for c in range(rows // ck):
            acc += jnp.dot(vt_ref[base + c], p[c * ck:(c + 1) * ck, :],
                           preferred_element_type=F32)
        o_ref[...] = (acc[:HEAD_DIM, :] / acc[HEAD_DIM:HEAD_DIM + 1, :]).T

    n_short = DIL_PAD // tq
    for t in range(n_short):
        pl.when(i == t)(functools.partial(attend, DIL_PAD - t * tq))
    pl.when(i >= n_short)(functools.partial(attend, 0))


def _dilated(qkv, bias, *, batch, seq, tq, ck):
    n = qkv.shape[0]
    nq = seq // tq
    win = DIL_PAD + tq
    vt_rows = HEAD_DIM + BF16_SUBLANES
    q0 = LATENT_COLS // HEAD_DIM
    return pl.pallas_call(
        functools.partial(_dil_kernel, tq=tq),
        grid=(batch, DIL_HEADS, nq),
        in_specs=[
            pl.BlockSpec((tq, HEAD_DIM), lambda b, h, i: (b * nq + i, q0 + h)),
            pl.BlockSpec((seq, HEAD_DIM), lambda b, h, i: (b, q0 + DIL_HEADS + h)),
            pl.BlockSpec((seq, HEAD_DIM), lambda b, h, i: (b, q0 + 2 * DIL_HEADS + h)),
            pl.BlockSpec((win, tq), lambda b, h, i: (0, 0)),
        ],
        out_specs=pl.BlockSpec((tq, HEAD_DIM), lambda b, h, i: (b * nq + i, h)),
        out_shape=jax.ShapeDtypeStruct((n, DIL_HEADS * HEAD_DIM), F32),
        scratch_shapes=[
            pltpu.VMEM((DIL_PAD + seq, HEAD_DIM), BF16),
            pltpu.VMEM(((DIL_PAD + seq) // ck, vt_rows, ck), BF16),
            pltpu.VMEM((win, tq), F32),
        ],
        compiler_params=_cparams(("arbitrary", "arbitrary", "arbitrary")),
        name="dilated_attention",
    )(qkv, qkv, qkv, bias)


def _out_proj_kernel(oa_ref, ob_ref, ga_ref, gb_ref, w_ref, x_ref, o_ref, h_ref):
    half = oa_ref.shape[1]

    @pl.when(pl.program_id(1) == 0)
    def _():
        h_ref[:, :half] = _rms(oa_ref[...].astype(F32), ga_ref[...]).astype(BF16)
        h_ref[:, half:] = _rms(ob_ref[...], gb_ref[...]).astype(BF16)

    o_ref[...] = x_ref[...] + jnp.dot(h_ref[...], w_ref[...], preferred_element_type=F32)


def _out_proj(o_a, o_b, g_a, g_b, w, x, *, tm, tn):
    n, d = x.shape
    half = o_a.shape[1]
    row = lambda i, j: (i, 0)
    return pl.pallas_call(
        _out_proj_kernel,
        grid=(n // tm, d // tn),
        in_specs=[pl.BlockSpec((tm, half), row)] * 2
        + [pl.BlockSpec((1, half), lambda i, j: (0, 0))] * 2
        + [pl.BlockSpec((2 * half, tn), lambda i, j: (0, j)),
           pl.BlockSpec((tm, tn), lambda i, j: (i, j))],
        out_specs=pl.BlockSpec((tm, tn), lambda i, j: (i, j)),
        out_shape=jax.ShapeDtypeStruct((n, d), F32),
        scratch_shapes=[pltpu.VMEM((tm, 2 * half), BF16)],
        compiler_params=_cparams(("arbitrary", "arbitrary")),
        name="out_proj",
    )(o_a, o_b, g_a.reshape(1, -1), g_b.reshape(1, -1), w, x)


def _gate_up_kernel(x_ref, g_ref, wg_ref, wu_ref, o_ref, h_ref):
    @pl.when(pl.program_id(1) == 0)
    def _():
        h_ref[...] = _rms(x_ref[...], g_ref[...]).astype(BF16)

    h = h_ref[...]
    gate = jnp.dot(h, wg_ref[...], preferred_element_type=F32)
    up = jnp.dot(h, wu_ref[...], preferred_element_type=F32)
    o_ref[...] = (gate * jax.nn.sigmoid(gate) * up).astype(o_ref.dtype)


def _gate_up(x, g, wg, wu, *, tm, tn):
    n, d = x.shape
    f = wg.shape[1]
    return pl.pallas_call(
        _gate_up_kernel,
        grid=(n // tm, f // tn),
        in_specs=[
            pl.BlockSpec((tm, d), lambda i, j: (i, 0)),
            pl.BlockSpec((1, d), lambda i, j: (0, 0)),
            pl.BlockSpec((d, tn), lambda i, j: (0, j)),
            pl.BlockSpec((d, tn), lambda i, j: (0, j)),
        ],
        out_specs=pl.BlockSpec((tm, tn), lambda i, j: (i, j)),
        out_shape=jax.ShapeDtypeStruct((n, f), BF16),
        scratch_shapes=[pltpu.VMEM((tm, d), BF16)],
        compiler_params=_cparams(("arbitrary", "arbitrary")),
        name="ffn_gate_up",
    )(x, g.reshape(1, -1), wg, wu)


def _down_kernel(a_ref, w_ref, x_ref, o_ref):
    o_ref[...] = x_ref[...] + jnp.dot(a_ref[...], w_ref[...], preferred_element_type=F32)


def _down(act, w, x, *, tm, tn):
    n, f = act.shape
    d = w.shape[1]
    return pl.pallas_call(
        _down_kernel,
        grid=(n // tm, d // tn),
        in_specs=[
            pl.BlockSpec((tm, f), lambda i, j: (i, 0)),
            pl.BlockSpec((f, tn), lambda i, j: (0, j)),
            pl.BlockSpec((tm, tn), lambda i, j: (i, j)),
        ],
        out_specs=pl.BlockSpec((tm, tn), lambda i, j: (i, j)),
        out_shape=jax.ShapeDtypeStruct((n, d), F32),
        compiler_params=_cparams(("arbitrary", "arbitrary")),
        name="ffn_down",
    )(act, w, x)


def _router_kernel(x_ref, g_ref, w_ref, idx_ref, wt_ref):
    h = _rms(x_ref[...], g_ref[...])
    logits = jnp.dot(h, w_ref[...], preferred_element_type=F32,
                     precision=lax.Precision.HIGHEST)
    lane_i = lax.broadcasted_iota(jnp.int32, logits.shape, 1)
    lane = lane_i.astype(F32)
    logits = jnp.where(lane_i < N_EXPERTS, logits, -jnp.inf)
    v1 = jnp.max(logits, axis=-1, keepdims=True)
    i1 = jnp.min(jnp.where(logits == v1, lane, float(LANES)), axis=-1, keepdims=True)
    rest = jnp.where(lane == i1, -jnp.inf, logits)
    v2 = jnp.max(rest, axis=-1, keepdims=True)
    i2 = jnp.min(jnp.where(rest == v2, lane, float(LANES)), axis=-1, keepdims=True)
    e2 = jnp.exp(v2 - v1)
    w1 = 1.0 / (1.0 + e2)
    w2 = e2 / (1.0 + e2)
    idx_ref[...] = jnp.where(lane_i == 0, i1, jnp.where(lane_i == 1, i2, 0.0)).astype(jnp.int32)
    wt_ref[...] = jnp.where(lane_i == 0, w1, jnp.where(lane_i == 1, w2, 0.0))


def _router(x, g, w_pad, *, tm):
    n, d = x.shape
    return pl.pallas_call(
        _router_kernel,
        grid=(n // tm,),
        in_specs=[
            pl.BlockSpec((tm, d), lambda i: (i, 0)),
            pl.BlockSpec((1, d), lambda i: (0, 0)),
            pl.BlockSpec((d, LANES), lambda i: (0, 0)),
        ],
        out_specs=[pl.BlockSpec((tm, LANES), lambda i: (i, 0))] * 2,
        out_shape=[jax.ShapeDtypeStruct((n, LANES), jnp.int32),
                   jax.ShapeDtypeStruct((n, LANES), F32)],
        compiler_params=_cparams(("arbitrary",)),
        name="moe_router",
    )(x, g.reshape(1, -1), w_pad)


def _row_copy(src_hbm, row, dst_vmem, slot, sem):
    return pltpu.make_async_copy(src_hbm.at[pl.ds(row, 1)], dst_vmem.at[pl.ds(slot, 1)], sem)


DMA_UNROLL = 8
DMA_PRIORITIES = 2


def _load_indices(idx_hbm, tile, idx_smem, isem):
    cp = pltpu.make_async_copy(idx_hbm.at[tile], idx_smem, isem)
    cp.start()
    cp.wait()


def _start_rows(idx_smem, src_hbm, dst_vmem, sem):
    def start(i, c):
        for prio in range(DMA_PRIORITIES):
            row = DMA_PRIORITIES * i + prio
            _row_copy(src_hbm, idx_smem[row], dst_vmem, row, sem).start(priority=prio)
        return c

    lax.fori_loop(0, dst_vmem.shape[0] // DMA_PRIORITIES, start, 0,
                  unroll=DMA_UNROLL // DMA_PRIORITIES)


def _wait_rows(src_hbm, dst_vmem, sem):
    def wait(i, c):
        _row_copy(src_hbm, 0, dst_vmem, i, sem).wait()
        return c

    lax.fori_loop(0, dst_vmem.shape[0], wait, 0, unroll=DMA_UNROLL)


def _gather_norm_kernel(nu_ref, idx_hbm, x_hbm, g_ref, o_ref, idx_smem, buf, isem, sem):
    i = pl.program_id(0)
    n_used = nu_ref[0]

    def fetch(tile, slot):
        _load_indices(idx_hbm, tile, idx_smem, isem)
        _start_rows(idx_smem, x_hbm, buf.at[slot], sem.at[slot])

    @pl.when((i == 0) & (n_used > 0))
    def _():
        fetch(0, 0)

    @pl.when(i + 1 < n_used)
    def _():
        fetch(i + 1, (i + 1) % 2)

    @pl.when(i < n_used)
    def _():
        slot = i % 2
        _wait_rows(x_hbm, buf.at[slot], sem.at[slot])
        o_ref[...] = _rms(buf[slot], g_ref[...]).astype(o_ref.dtype)

    @pl.when(i >= n_used)
    def _():
        o_ref[...] = jnp.zeros(o_ref.shape, o_ref.dtype)


def _gather_norm(n_used, row_token, x, g, *, tr):
    nt = row_token.shape[0]
    d = x.shape[1]
    grid_spec = pltpu.PrefetchScalarGridSpec(
        num_scalar_prefetch=1,
        grid=(nt,),
        in_specs=[
            pl.BlockSpec(memory_space=pl.ANY),
            pl.BlockSpec(memory_space=pl.ANY),
            pl.BlockSpec((1, d), lambda i, nu: (0, 0)),
        ],
        out_specs=pl.BlockSpec((tr, d), lambda i, nu: (i, 0)),
        scratch_shapes=[
            pltpu.SMEM((tr,), jnp.int32),
            pltpu.VMEM((2, tr, d), F32),
            pltpu.SemaphoreType.DMA,
            pltpu.SemaphoreType.DMA((2,)),
        ],
    )
    return pl.pallas_call(
        _gather_norm_kernel,
        grid_spec=grid_spec,
        out_shape=jax.ShapeDtypeStruct((nt * tr, d), BF16),
        compiler_params=_cparams(("arbitrary",)),
        name="moe_gather",
    )(n_used, row_token, x, g.reshape(1, -1))


def _moe_gate_up_kernel(te_ref, nu_ref, nxt_ref, x_ref, wg_hbm, wu_hbm, o_ref, stage_g, stage_u,
                        wg_bf, wu_bf, sem, *, layer, tn):
    j = pl.program_id(0)
    i = pl.program_id(1)
    n_col = pl.num_programs(0)
    used = i < nu_ref[0]
    e = te_ref[i]
    fresh = (i == 0) | (e != te_ref[jnp.maximum(i - 1, 0)])

    def window_copies(col, expert):
        cols = pl.ds(pl.multiple_of(col * tn, tn), tn)
        return (pltpu.make_async_copy(wg_hbm.at[layer, expert, :, cols], stage_g, sem.at[0]),
                pltpu.make_async_copy(wu_hbm.at[layer, expert, :, cols], stage_u, sem.at[1]))

    def start_window(col, expert):
        for cp in window_copies(col, expert):
            cp.start()

    @pl.when((j == 0) & (i == 0) & used)
    def _():
        start_window(0, e)

    @pl.when(used & fresh)
    def _():
        for cp in window_copies(j, e):
            cp.wait()
        wg_bf[...] = stage_g[...].astype(BF16)
        wu_bf[...] = stage_u[...].astype(BF16)
        nxt = nxt_ref[e]

        @pl.when(nxt >= 0)
        def _():
            start_window(j, nxt)

        @pl.when((nxt < 0) & (j + 1 < n_col))
        def _():
            start_window(j + 1, te_ref[0])

    @pl.when(used)
    def _():
        h = x_ref[...]
        gate = jnp.dot(h, wg_bf[...], preferred_element_type=F32)
        up = jnp.dot(h, wu_bf[...], preferred_element_type=F32)
        o_ref[...] = (gate * jax.nn.sigmoid(gate) * up).astype(o_ref.dtype)

    @pl.when(jnp.logical_not(used))
    def _():
        o_ref[...] = jnp.zeros(o_ref.shape, o_ref.dtype)


def _moe_gate_up(tile_expert, n_used, next_expert, xs, wg, wu, layer, *, tm, tn):
    r, d = xs.shape
    f = wg.shape[3]
    grid_spec = pltpu.PrefetchScalarGridSpec(
        num_scalar_prefetch=3,
        grid=(f // tn, r // tm),
        in_specs=[pl.BlockSpec((tm, d), lambda j, i, te, nu, nx: (i, 0)),
                  pl.BlockSpec(memory_space=pl.ANY),
                  pl.BlockSpec(memory_space=pl.ANY)],
        out_specs=pl.BlockSpec((tm, tn), lambda j, i, te, nu, nx: (i, j)),
        scratch_shapes=[pltpu.VMEM((d, tn), F32), pltpu.VMEM((d, tn), F32),
                        pltpu.VMEM((d, tn), BF16), pltpu.VMEM((d, tn), BF16),
                        pltpu.SemaphoreType.DMA((2,))],
    )
    return pl.pallas_call(
        functools.partial(_moe_gate_up_kernel, layer=layer, tn=tn),
        grid_spec=grid_spec,
        out_shape=jax.ShapeDtypeStruct((r, f), BF16),
        compiler_params=_cparams(("arbitrary", "arbitrary")),
        name="moe_gate_up",
    )(tile_expert, n_used, next_expert, xs, wg, wu)


def _moe_down_kernel(te_ref, nu_ref, a_ref, w_ref, o_ref):
    i = pl.program_id(1)

    @pl.when(i < nu_ref[0])
    def _():
        o_ref[...] = jnp.dot(a_ref[...], w_ref[0], preferred_element_type=F32)

    @pl.when(i >= nu_ref[0])
    def _():
        o_ref[...] = jnp.zeros(o_ref.shape, o_ref.dtype)


def _moe_down(tile_expert, n_used, act, w, layer, *, tm, tn):
    r, f = act.shape
    d = w.shape[3]
    grid_spec = pltpu.PrefetchScalarGridSpec(
        num_scalar_prefetch=2,
        grid=(d // tn, r // tm),
        in_specs=[
            pl.BlockSpec((tm, f), lambda j, i, te, nu: (i, 0)),
            pl.BlockSpec((None, 1, f, tn), lambda j, i, te, nu: (layer, te[i], 0, j)),
        ],
        out_specs=pl.BlockSpec((tm, tn), lambda j, i, te, nu: (i, j)),
    )
    return pl.pallas_call(
        _moe_down_kernel,
        grid_spec=grid_spec,
        out_shape=jax.ShapeDtypeStruct((r, d), F32),
        compiler_params=_cparams(("arbitrary", "arbitrary")),
        name="moe_down",
    )(tile_expert, n_used, act, w)


def _combine_kernel(p0_hbm, p1_hbm, ys_hbm, x_ref, wt_ref, g_ref, o_ref, i0_smem, i1_smem,
                    b0, b1, isem, sem0, sem1, *, final_norm):
    i = pl.program_id(0)
    n = pl.num_programs(0)

    def fetch(tile, slot):
        _load_indices(p0_hbm, tile, i0_smem, isem)
        _load_indices(p1_hbm, tile, i1_smem, isem)
        _start_rows(i0_smem, ys_hbm, b0.at[slot], sem0.at[slot])
        _start_rows(i1_smem, ys_hbm, b1.at[slot], sem1.at[slot])

    @pl.when(i == 0)
    def _():
        fetch(0, 0)

    @pl.when(i + 1 < n)
    def _():
        fetch(i + 1, (i + 1) % 2)

    slot = i % 2
    _wait_rows(ys_hbm, b0.at[slot], sem0.at[slot])
    _wait_rows(ys_hbm, b1.at[slot], sem1.at[slot])
    wt = wt_ref[...]
    y = x_ref[...] + (wt[:, 0:1] * b0[slot] + wt[:, 1:2] * b1[slot])
    if final_norm:
        y = _rms(y, g_ref[...])
    o_ref[...] = y


def _combine(pos0, pos1, ys, x, wt, g, *, tr, final_norm):
    n, d = x.shape
    return pl.pallas_call(
        functools.partial(_combine_kernel, final_norm=final_norm),
        grid=(n // tr,),
        in_specs=[
            pl.BlockSpec(memory_space=pl.ANY),
            pl.BlockSpec(memory_space=pl.ANY),
            pl.BlockSpec(memory_space=pl.ANY),
            pl.BlockSpec((tr, d), lambda i: (i, 0)),
            pl.BlockSpec((tr, LANES), lambda i: (i, 0)),
            pl.BlockSpec((1, d), lambda i: (0, 0)),
        ],
        out_specs=pl.BlockSpec((tr, d), lambda i: (i, 0)),
        out_shape=jax.ShapeDtypeStruct((n, d), F32),
        scratch_shapes=[
            pltpu.SMEM((tr,), jnp.int32),
            pltpu.SMEM((tr,), jnp.int32),
            pltpu.VMEM((2, tr, d), F32),
            pltpu.VMEM((2, tr, d), F32),
            pltpu.SemaphoreType.DMA,
            pltpu.SemaphoreType.DMA((2,)),
            pltpu.SemaphoreType.DMA((2,)),
        ],
        compiler_params=_cparams(("arbitrary",)),
        name="moe_combine",
    )(pos0.reshape(n // tr, tr), pos1.reshape(n // tr, tr), ys, x, wt, g.reshape(1, -1))


def _rmsnorm_kernel(x_ref, g_ref, o_ref):
    o_ref[...] = _rms(x_ref[...], g_ref[...])


def _final_norm(x, g, *, tm):
    n, d = x.shape
    return pl.pallas_call(
        _rmsnorm_kernel,
        grid=(n // tm,),
        in_specs=[pl.BlockSpec((tm, d), lambda i: (i, 0)),
                  pl.BlockSpec((1, d), lambda i: (0, 0))],
        out_specs=pl.BlockSpec((tm, d), lambda i: (i, 0)),
        out_shape=jax.ShapeDtypeStruct((n, d), F32),
        compiler_params=_cparams(("arbitrary",)),
        name="final_norm",
    )(x, g.reshape(1, -1))


def _routing_plan(top_idx, *, tm):
    n = top_idx.shape[0]
    n_assign = 2 * n
    rows = n_assign + N_EXPERTS * tm
    n_tiles = rows // tm
    e_flat = top_idx.reshape(n_assign)
    onehot = (e_flat[:, None] == jnp.arange(N_EXPERTS, dtype=jnp.int32)[None, :]).astype(jnp.int32)
    counts = jnp.sum(onehot, axis=0)
    rank = jnp.sum((jnp.cumsum(onehot, axis=0) - onehot) * onehot, axis=1)
    padded = ((counts + tm - 1) // tm) * tm
    ends = jnp.cumsum(padded)
    starts = ends - padded
    dest = starts[e_flat] + rank
    token = jnp.arange(n_assign, dtype=jnp.int32) // 2
    row_token = jnp.zeros((rows,), jnp.int32).at[dest].set(token)
    tile_start = jnp.arange(n_tiles, dtype=jnp.int32) * tm
    tile_expert = jnp.sum((tile_start[:, None] >= ends[None, :]).astype(jnp.int32), axis=1)
    tile_expert = jnp.minimum(tile_expert, N_EXPERTS - 1).astype(jnp.int32)
    n_used = (ends[-1] // tm).astype(jnp.int32).reshape(1)
    pos = dest.reshape(n, 2).astype(jnp.int32)
    experts = jnp.arange(N_EXPERTS, dtype=jnp.int32)
    later = (padded > 0)[None, :] & (experts[None, :] > experts[:, None])
    nxt = jnp.min(jnp.where(later, experts[None, :], N_EXPERTS), axis=1)
    next_expert = jnp.where(nxt == N_EXPERTS, -1, nxt).astype(jnp.int32)
    return row_token.reshape(n_tiles, tm), tile_expert, n_used, next_expert, pos


def _moe_layer(x, g, router_pad, wg, wu, wd, layer, final_g, *, final_norm):
    tm = 512
    idx, wt = _router(x, g, router_pad, tm=1024)
    row_token, tile_expert, n_used, next_expert, pos = _routing_plan(idx[:, :2], tm=tm)
    xs = _gather_norm(n_used, row_token, x, g, tr=tm)
    act = _moe_gate_up(tile_expert, n_used, next_expert, xs, wg, wu, layer, tm=tm, tn=1024)
    ys = _moe_down(tile_expert, n_used, act, wd, layer, tm=tm, tn=1024)
    return _combine(pos[:, 0], pos[:, 1], ys, x, wt, final_g, tr=256, final_norm=final_norm)


def _rope_tables(seq):
    pos = jnp.arange(seq, dtype=F32)[:, None]

    def tab(dim):
        inv = ROPE_THETA ** (-jnp.arange(0, dim, 2, dtype=F32) / dim)
        ang = pos * inv[None, :]
        return jnp.cos(ang), jnp.sin(ang)

    cf, sf = tab(HEAD_DIM)
    cos_f = jnp.concatenate([cf, cf], axis=1)
    sin_f = jnp.concatenate([-sf, sf], axis=1)
    cr, sr = tab(ROPE_DIM)
    zero = jnp.zeros((seq, LANES - ROPE_DIM), F32)
    cos_r = jnp.concatenate([cr, cr, zero], axis=1)
    sin_r = jnp.concatenate([-sr, sr, zero], axis=1)
    return cos_f, sin_f, cos_r, sin_r


def kernel(x, attn_norm_g, w_in, q_norm_g, kv_norm_g, w_uq, w_uk, w_uv, out_a_norm_g,
           out_b_norm_g, w_out, ffn_norm_g, dense_w_gate, dense_w_up, dense_w_down,
           router_w, moe_w_gate, moe_w_up, moe_w_down, final_norm_g):
    batch, seq, d_model = x.shape
    depth = w_in.shape[0]
    n = batch * seq
    mix = DIL_HEADS * HEAD_DIM
    cos_f, sin_f, cos_r, sin_r = _rope_tables(seq)
    dil_bias = _dil_bias(DIL_TQ)

    c0, c1, c2 = Q_LORA, Q_LORA + KV_LORA, Q_LORA + KV_LORA + ROPE_DIM
    lat_tiles = LATENT_COLS // PROJ_TN

    def proj_weight(wi):
        pad = jnp.zeros((d_model, 1024 - Q_LORA - ROPE_DIM), F32)
        return jnp.concatenate([wi[:, :c0], wi[:, c1:c2], pad, wi[:, c0:c1], wi[:, c2:]],
                               axis=1).astype(BF16)

    def q_up_weight(wq):
        uq = wq.reshape(Q_LORA, MLA_HEADS, QK_DIM)
        uq = jnp.concatenate([uq, jnp.zeros((Q_LORA, MLA_HEADS, 2 * LANES - QK_DIM), F32)], axis=2)
        return uq.reshape(Q_LORA, MLA_HEADS * 2 * LANES).astype(BF16)

    def bf16(w, idx):
        return w[idx].astype(BF16)

    moe_down_bf = moe_w_down.astype(BF16)

    xf = x.reshape(n, d_model)
    for layer in range(depth):
        g_attn = attn_norm_g[layer]
        proj = _norm_mm(xf, g_attn, proj_weight(w_in[layer]), cos_f, sin_f, x_block=(d_model, 0),
                        k_norm=d_model, tm=1024, tn=PROJ_TN,
                        rope_tiles=(lat_tiles, lat_tiles + 2 * mix // PROJ_TN),
                        scale_tiles=lat_tiles + mix // PROJ_TN,
                        scale=LOG2_E * HEAD_DIM ** -0.5, name="in_proj")
        q, kr = _q_up(proj, q_norm_g[layer], q_up_weight(w_uq[layer]), cos_r, sin_r, tm=1024,
                      tn=512, scale=LOG2_E * QK_DIM ** -0.5)
        w_ukv = jnp.concatenate([w_uk[layer], w_uv[layer]], axis=1).astype(BF16)
        kv = _norm_mm(proj, kv_norm_g[layer], w_ukv, cos_f, sin_f,
                      x_block=(KV_LORA, 1024 // KV_LORA), k_norm=KV_LORA, tm=1024, tn=1024,
                      name="kv_up")
        o_a = _mla_attention(q, kv, kr, batch=batch, seq=seq, tq=1024, tk=512)
        o_b = _dilated(proj, dil_bias, batch=batch, seq=seq, tq=DIL_TQ, ck=256)
        xf = _out_proj(o_a, o_b, out_a_norm_g[layer], out_b_norm_g[layer],
                       bf16(w_out, layer), xf, tm=1024, tn=512)
        j = layer // 2
        last = layer == depth - 1
        if layer % 2 == 0:
            act = _gate_up(xf, ffn_norm_g[layer], bf16(dense_w_gate, j), bf16(dense_w_up, j),
                           tm=1024, tn=512)
            xf = _down(act, bf16(dense_w_down, j), xf, tm=1024, tn=512)
            if last:
                xf = _final_norm(xf, final_norm_g, tm=1024)
        else:
            router_pad = jnp.concatenate(
                [router_w[j], jnp.zeros((d_model, LANES - N_EXPERTS), F32)], axis=1)
            xf = _moe_layer(xf, ffn_norm_g[layer], router_pad, moe_w_gate, moe_w_up, moe_down_bf,
                            j, final_norm_g, final_norm=last)
    return xf.reshape(batch, seq, d_model)
```

```python
import functools

import jax
import jax.numpy as jnp
from jax import lax
from jax.experimental import pallas as pl
from jax.experimental.pallas import tpu as pltpu

F32 = jnp.float32
BF16 = jnp.bfloat16

LANES = 128
BF16_SUBLANES = 16
MLA_HEADS = 8
DIL_HEADS = 8
HEAD_DIM = 128
Q_LORA = 768
KV_LORA = 512
ROPE_DIM = 64
QK_DIM = HEAD_DIM + ROPE_DIM
N_EXPERTS = 8
DIL_CONFIGS = ((128, 1), (512, 4), (2048, 16))
DIL_PAD = max(w for w, _ in DIL_CONFIGS)
DIL_TQ = 512
LATENT_COLS = 1536
PROJ_TN = 512
ROPE_THETA = 10000.0
EPS = 1e-6
NEG_INF = -1e30
LOG2_E = 1.4426950408889634
VMEM_LIMIT = 56 * 1024 * 1024


def _cparams(sem):
    return pltpu.CompilerParams(dimension_semantics=sem, vmem_limit_bytes=VMEM_LIMIT)


def _rms(xf, g):
    ms = jnp.mean(xf * xf, axis=-1, keepdims=True)
    return xf * lax.rsqrt(ms + EPS) * g


def _rope_full(a, c, s):
    return a * c + pltpu.roll(a, HEAD_DIM // 2, 1) * s


def _rope_half(a, c, s):
    lane = lax.broadcasted_iota(jnp.int32, a.shape, 1)
    lower = (lane % ROPE_DIM) < (ROPE_DIM // 2)
    partner = jnp.where(lower, pltpu.roll(a, LANES - ROPE_DIM // 2, 1),
                        pltpu.roll(a, ROPE_DIM // 2, 1))
    return a * c + partner * s


def _norm_mm_kernel(x_ref, g_ref, w_ref, cos_ref, sin_ref, o_ref, h_ref, *,
                    k_norm, rope_tiles, scale_tiles, scale):
    j = pl.program_id(1)

    @pl.when(j == 0)
    def _():
        xf = x_ref[:, :k_norm].astype(F32)
        h_ref[...] = _rms(xf, g_ref[...]).astype(BF16)

    acc = jnp.dot(h_ref[...], w_ref[...], preferred_element_type=F32)
    if rope_tiles is None:
        o_ref[...] = acc.astype(o_ref.dtype)
        return
    rope = (j >= rope_tiles[0]) & (j < rope_tiles[1])

    @pl.when(rope)
    def _():
        scaled = j < scale_tiles
        c = jnp.where(scaled, cos_ref[...] * scale, cos_ref[...])
        s = jnp.where(scaled, sin_ref[...] * scale, sin_ref[...])
        for k in range(acc.shape[1] // HEAD_DIM):
            sl = slice(k * HEAD_DIM, (k + 1) * HEAD_DIM)
            o_ref[:, sl] = _rope_full(acc[:, sl], c, s).astype(o_ref.dtype)

    @pl.when(jnp.logical_not(rope))
    def _():
        o_ref[...] = acc.astype(o_ref.dtype)


def _norm_mm(x, g, w, cos, sin, *, x_block, k_norm, tm, tn, rope_tiles=None, scale_tiles=0,
             scale=1.0, name):
    n = x.shape[0]
    xw, xb = x_block
    kw, nout = w.shape
    assert kw == k_norm and n % tm == 0 and nout % tn == 0
    ns = cos.shape[0] // tm
    kern = functools.partial(_norm_mm_kernel, k_norm=k_norm, rope_tiles=rope_tiles,
                             scale_tiles=scale_tiles, scale=scale)
    return pl.pallas_call(
        kern,
        grid=(n // tm, nout // tn),
        in_specs=[
            pl.BlockSpec((tm, xw), lambda i, j: (i, xb)),
            pl.BlockSpec((1, k_norm), lambda i, j: (0, 0)),
            pl.BlockSpec((k_norm, tn), lambda i, j: (0, j)),
            pl.BlockSpec((tm, HEAD_DIM), lambda i, j: (i % ns, 0)),
            pl.BlockSpec((tm, HEAD_DIM), lambda i, j: (i % ns, 0)),
        ],
        out_specs=pl.BlockSpec((tm, tn), lambda i, j: (i, j)),
        out_shape=jax.ShapeDtypeStruct((n, nout), BF16),
        scratch_shapes=[pltpu.VMEM((tm, k_norm), BF16)],
        compiler_params=_cparams(("arbitrary", "arbitrary")),
        name=name,
    )(x, g.reshape(1, -1), w, cos, sin)


def _q_up_kernel(x_ref, g_ref, w_ref, cos_ref, sin_ref, q_ref, kr_ref, h_ref, *, scale):
    j = pl.program_id(1)
    c = cos_ref[...]
    s = sin_ref[...]

    @pl.when(j == 0)
    def _():
        blk = x_ref[...].astype(F32)
        h_ref[...] = _rms(blk[:, :Q_LORA], g_ref[...]).astype(BF16)
        kr_ref[...] = _rope_half(blk[:, Q_LORA:Q_LORA + LANES], c, s).astype(BF16)

    acc = jnp.dot(h_ref[...], w_ref[...], preferred_element_type=F32)
    for hh in range(acc.shape[1] // (2 * LANES)):
        lo = hh * 2 * LANES
        q_ref[:, lo:lo + LANES] = (acc[:, lo:lo + LANES] * scale).astype(BF16)
        q_ref[:, lo + LANES:lo + 2 * LANES] = (
            _rope_half(acc[:, lo + LANES:lo + 2 * LANES], c, s) * scale).astype(BF16)


def _q_up(lat, g, w, cos, sin, *, tm, tn, scale):
    n = lat.shape[0]
    nout = w.shape[1]
    ns = cos.shape[0] // tm
    return pl.pallas_call(
        functools.partial(_q_up_kernel, scale=scale),
        grid=(n // tm, nout // tn),
        in_specs=[
            pl.BlockSpec((tm, 1024), lambda i, j: (i, 0)),
            pl.BlockSpec((1, Q_LORA), lambda i, j: (0, 0)),
            pl.BlockSpec((Q_LORA, tn), lambda i, j: (0, j)),
            pl.BlockSpec((tm, LANES), lambda i, j: (i % ns, 0)),
            pl.BlockSpec((tm, LANES), lambda i, j: (i % ns, 0)),
        ],
        out_specs=[
            pl.BlockSpec((tm, tn), lambda i, j: (i, j)),
            pl.BlockSpec((tm, LANES), lambda i, j: (i, 0)),
        ],
        out_shape=[jax.ShapeDtypeStruct((n, nout), BF16),
                   jax.ShapeDtypeStruct((n, LANES), BF16)],
        scratch_shapes=[pltpu.VMEM((tm, Q_LORA), BF16)],
        compiler_params=_cparams(("arbitrary", "arbitrary")),
        name="q_up",
    )(lat, g.reshape(1, -1), w, cos, sin)


def _mla_kernel(q_ref, kn_ref, kr_ref, v_ref, o_ref, kcat_ref, vt_ref, qt_ref, m_ref,
                acc_ref, s_ref, *, tq, tk):
    qi = pl.program_id(2)
    seq = kcat_ref.shape[0]
    vt_rows = vt_ref.shape[1]

    @pl.when(qi == 0)
    def _():
        kcat_ref[:, :HEAD_DIM] = kn_ref[...]
        kcat_ref[:, HEAD_DIM:] = kr_ref[...]
        ones_row = (lax.broadcasted_iota(jnp.int32, (vt_rows - HEAD_DIM, tk), 0) == 0)

        def transpose_chunk(c, carry):
            off = pl.multiple_of(c * tk, tk)
            vt_ref[c, :HEAD_DIM, :] = v_ref[pl.ds(off, tk), :].astype(F32).T.astype(BF16)
            vt_ref[c, HEAD_DIM:, :] = ones_row.astype(BF16)
            return carry

        lax.fori_loop(0, seq // tk, transpose_chunk, 0)

    qt_ref[...] = q_ref[...].astype(F32).T.astype(BF16)
    m_ref[...] = jnp.full(m_ref.shape, NEG_INF, F32)
    acc_ref[...] = jnp.zeros(acc_ref.shape, F32)
    per_q = tq // tk

    def scores(j, lo):
        off = pl.multiple_of(j * tk, tk)
        return jnp.dot(kcat_ref[pl.ds(off, tk), :], qt_ref[:, lo:],
                       preferred_element_type=F32)

    def chunk(j, masked, lo, next_lo):
        s = s_ref[j % 2, :, lo:]
        if next_lo is not None:
            s_ref[(j + 1) % 2, :, next_lo:] = scores(j + 1, next_lo)
        if masked:
            key = j * tk + lax.broadcasted_iota(jnp.int32, s.shape, 0)
            qry = qi * tq + lo + lax.broadcasted_iota(jnp.int32, s.shape, 1)
            s = jnp.where(key <= qry, s, NEG_INF)
        m_old = m_ref[:, lo:]
        m_new = jnp.maximum(m_old, jnp.max(s, axis=0, keepdims=True))
        a = jnp.exp2(m_old - m_new)
        p = jnp.exp2(s - m_new).astype(BF16)
        acc_ref[:, lo:] = a * acc_ref[:, lo:] + jnp.dot(vt_ref[j], p,
                                                        preferred_element_type=F32)
        m_ref[:, lo:] = m_new

    def body(j, carry):
        chunk(j, False, 0, 0)
        return carry

    n_full = qi * per_q
    s_ref[0] = scores(0, 0)
    lax.fori_loop(0, n_full, body, 0)
    for d in range(per_q):
        chunk(n_full + d, True, d * tk, (d + 1) * tk if d + 1 < per_q else None)
    out_t = acc_ref[:HEAD_DIM, :] / acc_ref[HEAD_DIM:HEAD_DIM + 1, :]
    o_ref[...] = out_t.T.astype(o_ref.dtype)


def _mla_attention(q, kv, kr, *, batch, seq, tq, tk):
    n = q.shape[0]
    nq = seq // tq
    return pl.pallas_call(
        functools.partial(_mla_kernel, tq=tq, tk=tk),
        grid=(batch, MLA_HEADS, nq),
        in_specs=[
            pl.BlockSpec((tq, 2 * LANES), lambda b, h, i: (b * nq + i, h)),
            pl.BlockSpec((seq, HEAD_DIM), lambda b, h, i: (b, h)),
            pl.BlockSpec((seq, LANES), lambda b, h, i: (b, 0)),
            pl.BlockSpec((seq, HEAD_DIM), lambda b, h, i: (b, MLA_HEADS + h)),
        ],
        out_specs=pl.BlockSpec((tq, HEAD_DIM), lambda b, h, i: (b * nq + i, h)),
        out_shape=jax.ShapeDtypeStruct((n, MLA_HEADS * HEAD_DIM), BF16),
        scratch_shapes=[
            pltpu.VMEM((seq, 2 * LANES), BF16),
            pltpu.VMEM((seq // tk, HEAD_DIM + BF16_SUBLANES, tk), BF16),
            pltpu.VMEM((2 * LANES, tq), BF16),
            pltpu.VMEM((1, tq), F32),
            pltpu.VMEM((HEAD_DIM + BF16_SUBLANES, tq), F32),
            pltpu.VMEM((2, tk, tq), F32),
        ],
        compiler_params=_cparams(("arbitrary", "arbitrary", "arbitrary")),
        name="mla_attention",
    )(q, kv, kr, kv)


def _dil_bias(tq):
    kk = jnp.arange(DIL_PAD + tq, dtype=jnp.int32)[:, None]
    qq = jnp.arange(tq, dtype=jnp.int32)[None, :]
    delta = qq + DIL_PAD - kk
    count = jnp.zeros(delta.shape, F32)
    for win, dil in DIL_CONFIGS:
        count += ((delta >= 0) & (delta <= win) & (delta % dil == 0)).astype(F32)
    return jnp.where(count > 0, jnp.log2(jnp.maximum(count, 1.0)), NEG_INF)


def _dil_kernel(q_ref, k_ref, v_ref, bias_ref, o_ref, kpad_ref, vt_ref, s_ref, *, tq):
    i = pl.program_id(2)
    seq = k_ref.shape[0]
    ck = vt_ref.shape[2]
    vt_rows = vt_ref.shape[1]
    n_pad = DIL_PAD // ck
    win = DIL_PAD + tq

    @pl.when(i == 0)
    def _():
        kpad_ref[:DIL_PAD, :] = jnp.zeros((DIL_PAD, HEAD_DIM), BF16)
        kpad_ref[DIL_PAD:, :] = k_ref[...]
        ones_row = (lax.broadcasted_iota(jnp.int32, (vt_rows - HEAD_DIM, ck), 0) == 0)
        for c in range(n_pad):
            vt_ref[c] = jnp.zeros((vt_rows, ck), BF16)

        def transpose_chunk(c, carry):
            off = pl.multiple_of(c * ck, ck)
            vt_ref[n_pad + c, :HEAD_DIM, :] = v_ref[pl.ds(off, ck), :].astype(F32).T.astype(BF16)
            vt_ref[n_pad + c, HEAD_DIM:, :] = ones_row.astype(BF16)
            return carry

        lax.fori_loop(0, seq // ck, transpose_chunk, 0)

    s0 = pl.multiple_of(i * tq, tq)
    qt = q_ref[...].astype(F32).T.astype(BF16)

    def attend(skip):
        rows = win - skip
        s_ref[:rows, :] = jnp.dot(kpad_ref[pl.ds(s0 + skip, rows), :], qt,
                                  preferred_element_type=F32) + bias_ref[skip:, :]
        s = s_ref[:rows, :]
        m = jnp.max(s, axis=0, keepdims=True)
        p = jnp.exp2(s - m).astype(BF16)
        base = i * (tq // ck) + skip // ck
        acc = jnp.zeros((vt_rows, tq), F32)
        for c in range(rows // ck):
            acc += jnp.dot(vt_ref[base + c], p[c * ck:(c + 1) * ck, :],
                           preferred_element_type=F32)
        o_ref[...] = (acc[:HEAD_DIM, :] / acc[HEAD_DIM:HEAD_DIM + 1, :]).T

    n_short = DIL_PAD // tq
    for t in range(n_short):
        pl.when(i == t)(functools.partial(attend, DIL_PAD - t * tq))
    pl.when(i >= n_short)(functools.partial(attend, 0))


def _dilated(qkv, bias, *, batch, seq, tq, ck):
    n = qkv.shape[0]
    nq = seq // tq
    win = DIL_PAD + tq
    vt_rows = HEAD_DIM + BF16_SUBLANES
    q0 = LATENT_COLS // HEAD_DIM
    return pl.pallas_call(
        functools.partial(_dil_kernel, tq=tq),
        grid=(batch, DIL_HEADS, nq),
        in_specs=[
            pl.BlockSpec((tq, HEAD_DIM), lambda b, h, i: (b * nq + i, q0 + h)),
            pl.BlockSpec((seq, HEAD_DIM), lambda b, h, i: (b, q0 + DIL_HEADS + h)),
            pl.BlockSpec((seq, HEAD_DIM), lambda b, h, i: (b, q0 + 2 * DIL_HEADS + h)),
            pl.BlockSpec((win, tq), lambda b, h, i: (0, 0)),
        ],
        out_specs=pl.BlockSpec((tq, HEAD_DIM), lambda b, h, i: (b * nq + i, h)),
        out_shape=jax.ShapeDtypeStruct((n, DIL_HEADS * HEAD_DIM), F32),
        scratch_shapes=[
            pltpu.VMEM((DIL_PAD + seq, HEAD_DIM), BF16),
            pltpu.VMEM(((DIL_PAD + seq) // ck, vt_rows, ck), BF16),
            pltpu.VMEM((win, tq), F32),
        ],
        compiler_params=_cparams(("arbitrary", "arbitrary", "arbitrary")),
        name="dilated_attention",
    )(qkv, qkv, qkv, bias)


def _out_proj_kernel(oa_ref, ob_ref, ga_ref, gb_ref, w_ref, x_ref, o_ref, h_ref):
    half = oa_ref.shape[1]

    @pl.when(pl.program_id(1) == 0)
    def _():
        h_ref[:, :half] = _rms(oa_ref[...].astype(F32), ga_ref[...]).astype(BF16)
        h_ref[:, half:] = _rms(ob_ref[...], gb_ref[...]).astype(BF16)

    o_ref[...] = x_ref[...] + jnp.dot(h_ref[...], w_ref[...], preferred_element_type=F32)


def _out_proj(o_a, o_b, g_a, g_b, w, x, *, tm, tn):
    n, d = x.shape
    half = o_a.shape[1]
    row = lambda i, j: (i, 0)
    return pl.pallas_call(
        _out_proj_kernel,
        grid=(n // tm, d // tn),
        in_specs=[pl.BlockSpec((tm, half), row)] * 2
        + [pl.BlockSpec((1, half), lambda i, j: (0, 0))] * 2
        + [pl.BlockSpec((2 * half, tn), lambda i, j: (0, j)),
           pl.BlockSpec((tm, tn), lambda i, j: (i, j))],
        out_specs=pl.BlockSpec((tm, tn), lambda i, j: (i, j)),
        out_shape=jax.ShapeDtypeStruct((n, d), F32),
        scratch_shapes=[pltpu.VMEM((tm, 2 * half), BF16)],
        compiler_params=_cparams(("arbitrary", "arbitrary")),
        name="out_proj",
    )(o_a, o_b, g_a.reshape(1, -1), g_b.reshape(1, -1), w, x)


def _gate_up_kernel(x_ref, g_ref, wg_ref, wu_ref, o_ref, h_ref):
    @pl.when(pl.program_id(1) == 0)
    def _():
        h_ref[...] = _rms(x_ref[...], g_ref[...]).astype(BF16)

    h = h_ref[...]
    gate = jnp.dot(h, wg_ref[...], preferred_element_type=F32)
    up = jnp.dot(h, wu_ref[...], preferred_element_type=F32)
    o_ref[...] = (gate * jax.nn.sigmoid(gate) * up).astype(o_ref.dtype)


def _gate_up(x, g, wg, wu, *, tm, tn):
    n, d = x.shape
    f = wg.shape[1]
    return pl.pallas_call(
        _gate_up_kernel,
        grid=(n // tm, f // tn),
        in_specs=[
            pl.BlockSpec((tm, d), lambda i, j: (i, 0)),
            pl.BlockSpec((1, d), lambda i, j: (0, 0)),
            pl.BlockSpec((d, tn), lambda i, j: (0, j)),
            pl.BlockSpec((d, tn), lambda i, j: (0, j)),
        ],
        out_specs=pl.BlockSpec((tm, tn), lambda i, j: (i, j)),
        out_shape=jax.ShapeDtypeStruct((n, f), BF16),
        scratch_shapes=[pltpu.VMEM((tm, d), BF16)],
        compiler_params=_cparams(("arbitrary", "arbitrary")),
        name="ffn_gate_up",
    )(x, g.reshape(1, -1), wg, wu)


def _down_kernel(a_ref, w_ref, x_ref, o_ref):
    o_ref[...] = x_ref[...] + jnp.dot(a_ref[...], w_ref[...], preferred_element_type=F32)


def _down(act, w, x, *, tm, tn):
    n, f = act.shape
    d = w.shape[1]
    return pl.pallas_call(
        _down_kernel,
        grid=(n // tm, d // tn),
        in_specs=[
            pl.BlockSpec((tm, f), lambda i, j: (i, 0)),
            pl.BlockSpec((f, tn), lambda i, j: (0, j)),
            pl.BlockSpec((tm, tn), lambda i, j: (i, j)),
        ],
        out_specs=pl.BlockSpec((tm, tn), lambda i, j: (i, j)),
        out_shape=jax.ShapeDtypeStruct((n, d), F32),
        compiler_params=_cparams(("arbitrary", "arbitrary")),
        name="ffn_down",
    )(act, w, x)


def _router_kernel(x_ref, g_ref, w_ref, idx_ref, wt_ref):
    h = _rms(x_ref[...], g_ref[...])
    logits = jnp.dot(h, w_ref[...], preferred_element_type=F32,
                     precision=lax.Precision.HIGHEST)
    lane_i = lax.broadcasted_iota(jnp.int32, logits.shape, 1)
    lane = lane_i.astype(F32)
    logits = jnp.where(lane_i < N_EXPERTS, logits, -jnp.inf)
    v1 = jnp.max(logits, axis=-1, keepdims=True)
    i1 = jnp.min(jnp.where(logits == v1, lane, float(LANES)), axis=-1, keepdims=True)
    rest = jnp.where(lane == i1, -jnp.inf, logits)
    v2 = jnp.max(rest, axis=-1, keepdims=True)
    i2 = jnp.min(jnp.where(rest == v2, lane, float(LANES)), axis=-1, keepdims=True)
    e2 = jnp.exp(v2 - v1)
    w1 = 1.0 / (1.0 + e2)
    w2 = e2 / (1.0 + e2)
    idx_ref[...] = jnp.where(lane_i == 0, i1, jnp.where(lane_i == 1, i2, 0.0)).astype(jnp.int32)
    wt_ref[...] = jnp.where(lane_i == 0, w1, jnp.where(lane_i == 1, w2, 0.0))


def _router(x, g, w_pad, *, tm):
    n, d = x.shape
    return pl.pallas_call(
        _router_kernel,
        grid=(n // tm,),
        in_specs=[
            pl.BlockSpec((tm, d), lambda i: (i, 0)),
            pl.BlockSpec((1, d), lambda i: (0, 0)),
            pl.BlockSpec((d, LANES), lambda i: (0, 0)),
        ],
        out_specs=[pl.BlockSpec((tm, LANES), lambda i: (i, 0))] * 2,
        out_shape=[jax.ShapeDtypeStruct((n, LANES), jnp.int32),
                   jax.ShapeDtypeStruct((n, LANES), F32)],
        compiler_params=_cparams(("arbitrary",)),
        name="moe_router",
    )(x, g.reshape(1, -1), w_pad)


def _row_copy(src_hbm, row, dst_vmem, slot, sem):
    return pltpu.make_async_copy(src_hbm.at[pl.ds(row, 1)], dst_vmem.at[pl.ds(slot, 1)], sem)


DMA_UNROLL = 8
DMA_PRIORITIES = 2


def _load_indices(idx_hbm, tile, idx_smem, isem):
    cp = pltpu.make_async_copy(idx_hbm.at[tile], idx_smem, isem)
    cp.start()
    cp.wait()


def _start_rows(idx_smem, src_hbm, dst_vmem, sem):
    def start(i, c):
        for prio in range(DMA_PRIORITIES):
            row = DMA_PRIORITIES * i + prio
            _row_copy(src_hbm, idx_smem[row], dst_vmem, row, sem).start(priority=prio)
        return c

    lax.fori_loop(0, dst_vmem.shape[0] // DMA_PRIORITIES, start, 0,
                  unroll=DMA_UNROLL // DMA_PRIORITIES)


def _wait_rows(src_hbm, dst_vmem, sem):
    def wait(i, c):
        _row_copy(src_hbm, 0, dst_vmem, i, sem).wait()
        return c

    lax.fori_loop(0, dst_vmem.shape[0], wait, 0, unroll=DMA_UNROLL)


def _gather_norm_kernel(nu_ref, idx_hbm, x_hbm, g_ref, o_ref, idx_smem, buf, isem, sem):
    i = pl.program_id(0)
    n_used = nu_ref[0]

    def fetch(tile, slot):
        _load_indices(idx_hbm, tile, idx_smem, isem)
        _start_rows(idx_smem, x_hbm, buf.at[slot], sem.at[slot])

    @pl.when((i == 0) & (n_used > 0))
    def _():
        fetch(0, 0)

    @pl.when(i + 1 < n_used)
    def _():
        fetch(i + 1, (i + 1) % 2)

    @pl.when(i < n_used)
    def _():
        slot = i % 2
        _wait_rows(x_hbm, buf.at[slot], sem.at[slot])
        o_ref[...] = _rms(buf[slot], g_ref[...]).astype(o_ref.dtype)

    @pl.when(i >= n_used)
    def _():
        o_ref[...] = jnp.zeros(o_ref.shape, o_ref.dtype)


def _gather_norm(n_used, row_token, x, g, *, tr):
    nt = row_token.shape[0]
    d = x.shape[1]
    grid_spec = pltpu.PrefetchScalarGridSpec(
        num_scalar_prefetch=1,
        grid=(nt,),
        in_specs=[
            pl.BlockSpec(memory_space=pl.ANY),
            pl.BlockSpec(memory_space=pl.ANY),
            pl.BlockSpec((1, d), lambda i, nu: (0, 0)),
        ],
        out_specs=pl.BlockSpec((tr, d), lambda i, nu: (i, 0)),
        scratch_shapes=[
            pltpu.SMEM((tr,), jnp.int32),
            pltpu.VMEM((2, tr, d), F32),
            pltpu.SemaphoreType.DMA,
            pltpu.SemaphoreType.DMA((2,)),
        ],
    )
    return pl.pallas_call(
        _gather_norm_kernel,
        grid_spec=grid_spec,
        out_shape=jax.ShapeDtypeStruct((nt * tr, d), BF16),
        compiler_params=_cparams(("arbitrary",)),
        name="moe_gather",
    )(n_used, row_token, x, g.reshape(1, -1))


def _moe_gate_up_kernel(te_ref, nu_ref, nxt_ref, x_ref, wg_hbm, wu_hbm, o_ref, stage_g, stage_u,
                        wg_bf, wu_bf, sem, *, layer, tn):
    j = pl.program_id(0)
    i = pl.program_id(1)
    n_col = pl.num_programs(0)
    used = i < nu_ref[0]
    e = te_ref[i]
    fresh = (i == 0) | (e != te_ref[jnp.maximum(i - 1, 0)])

    def window_copies(col, expert):
        cols = pl.ds(pl.multiple_of(col * tn, tn), tn)
        return (pltpu.make_async_copy(wg_hbm.at[layer, expert, :, cols], stage_g, sem.at[0]),
                pltpu.make_async_copy(wu_hbm.at[layer, expert, :, cols], stage_u, sem.at[1]))

    def start_window(col, expert):
        for cp in window_copies(col, expert):
            cp.start()

    @pl.when((j == 0) & (i == 0) & used)
    def _():
        start_window(0, e)

    @pl.when(used & fresh)
    def _():
        for cp in window_copies(j, e):
            cp.wait()
        wg_bf[...] = stage_g[...].astype(BF16)
        wu_bf[...] = stage_u[...].astype(BF16)
        nxt = nxt_ref[e]

        @pl.when(nxt >= 0)
        def _():
            start_window(j, nxt)

        @pl.when((nxt < 0) & (j + 1 < n_col))
        def _():
            start_window(j + 1, te_ref[0])

    @pl.when(used)
    def _():
        h = x_ref[...]
        gate = jnp.dot(h, wg_bf[...], preferred_element_type=F32)
        up = jnp.dot(h, wu_bf[...], preferred_element_type=F32)
        o_ref[...] = (gate * jax.nn.sigmoid(gate) * up).astype(o_ref.dtype)

    @pl.when(jnp.logical_not(used))
    def _():
        o_ref[...] = jnp.zeros(o_ref.shape, o_ref.dtype)


def _moe_gate_up(tile_expert, n_used, next_expert, xs, wg, wu, layer, *, tm, tn):
    r, d = xs.shape
    f = wg.shape[3]
    grid_spec = pltpu.PrefetchScalarGridSpec(
        num_scalar_prefetch=3,
        grid=(f // tn, r // tm),
        in_specs=[pl.BlockSpec((tm, d), lambda j, i, te, nu, nx: (i, 0)),
                  pl.BlockSpec(memory_space=pl.ANY),
                  pl.BlockSpec(memory_space=pl.ANY)],
        out_specs=pl.BlockSpec((tm, tn), lambda j, i, te, nu, nx: (i, j)),
        scratch_shapes=[pltpu.VMEM((d, tn), F32), pltpu.VMEM((d, tn), F32),
                        pltpu.VMEM((d, tn), BF16), pltpu.VMEM((d, tn), BF16),
                        pltpu.SemaphoreType.DMA((2,))],
    )
    return pl.pallas_call(
        functools.partial(_moe_gate_up_kernel, layer=layer, tn=tn),
        grid_spec=grid_spec,
        out_shape=jax.ShapeDtypeStruct((r, f), BF16),
        compiler_params=_cparams(("arbitrary", "arbitrary")),
        name="moe_gate_up",
    )(tile_expert, n_used, next_expert, xs, wg, wu)


def _moe_down_kernel(te_ref, nu_ref, a_ref, w_ref, o_ref):
    i = pl.program_id(1)

    @pl.when(i < nu_ref[0])
    def _():
        o_ref[...] = jnp.dot(a_ref[...], w_ref[0], preferred_element_type=F32)

    @pl.when(i >= nu_ref[0])
    def _():
        o_ref[...] = jnp.zeros(o_ref.shape, o_ref.dtype)


def _moe_down(tile_expert, n_used, act, w, layer, *, tm, tn):
    r, f = act.shape
    d = w.shape[3]
    grid_spec = pltpu.PrefetchScalarGridSpec(
        num_scalar_prefetch=2,
        grid=(d // tn, r // tm),
        in_specs=[
            pl.BlockSpec((tm, f), lambda j, i, te, nu: (i, 0)),
            pl.BlockSpec((None, 1, f, tn), lambda j, i, te, nu: (layer, te[i], 0, j)),
        ],
        out_specs=pl.BlockSpec((tm, tn), lambda j, i, te, nu: (i, j)),
    )
    return pl.pallas_call(
        _moe_down_kernel,
        grid_spec=grid_spec,
        out_shape=jax.ShapeDtypeStruct((r, d), F32),
        compiler_params=_cparams(("arbitrary", "arbitrary")),
        name="moe_down",
    )(tile_expert, n_used, act, w)


def _combine_kernel(p0_hbm, p1_hbm, ys_hbm, x_ref, wt_ref, g_ref, o_ref, i0_smem, i1_smem,
                    b0, b1, isem, sem0, sem1, *, final_norm):
    i = pl.program_id(0)
    n = pl.num_programs(0)

    def fetch(tile, slot):
        loads = (pltpu.make_async_copy(p0_hbm.at[tile], i0_smem, isem.at[0]),
                 pltpu.make_async_copy(p1_hbm.at[tile], i1_smem, isem.at[1]))
        for cp in loads:
            cp.start()
        for cp in loads:
            cp.wait()
        _start_rows(i0_smem, ys_hbm, b0.at[slot], sem0.at[slot])
        _start_rows(i1_smem, ys_hbm, b1.at[slot], sem1.at[slot])

    @pl.when(i == 0)
    def _():
        fetch(0, 0)

    @pl.when(i + 1 < n)
    def _():
        fetch(i + 1, (i + 1) % 2)

    slot = i % 2
    _wait_rows(ys_hbm, b0.at[slot], sem0.at[slot])
    _wait_rows(ys_hbm, b1.at[slot], sem1.at[slot])
    wt = wt_ref[...]
    y = x_ref[...] + (wt[:, 0:1] * b0[slot] + wt[:, 1:2] * b1[slot])
    if final_norm:
        y = _rms(y, g_ref[...])
    o_ref[...] = y


def _combine(pos0, pos1, ys, x, wt, g, *, tr, final_norm):
    n, d = x.shape
    return pl.pallas_call(
        functools.partial(_combine_kernel, final_norm=final_norm),
        grid=(n // tr,),
        in_specs=[
            pl.BlockSpec(memory_space=pl.ANY),
            pl.BlockSpec(memory_space=pl.ANY),
            pl.BlockSpec(memory_space=pl.ANY),
            pl.BlockSpec((tr, d), lambda i: (i, 0)),
            pl.BlockSpec((tr, LANES), lambda i: (i, 0)),
            pl.BlockSpec((1, d), lambda i: (0, 0)),
        ],
        out_specs=pl.BlockSpec((tr, d), lambda i: (i, 0)),
        out_shape=jax.ShapeDtypeStruct((n, d), F32),
        scratch_shapes=[
            pltpu.SMEM((tr,), jnp.int32),
            pltpu.SMEM((tr,), jnp.int32),
            pltpu.VMEM((2, tr, d), F32),
            pltpu.VMEM((2, tr, d), F32),
            pltpu.SemaphoreType.DMA((2,)),
            pltpu.SemaphoreType.DMA((2,)),
            pltpu.SemaphoreType.DMA((2,)),
        ],
        compiler_params=_cparams(("arbitrary",)),
        name="moe_combine",
    )(pos0.reshape(n // tr, tr), pos1.reshape(n // tr, tr), ys, x, wt, g.reshape(1, -1))


def _rmsnorm_kernel(x_ref, g_ref, o_ref):
    o_ref[...] = _rms(x_ref[...], g_ref[...])


def _final_norm(x, g, *, tm):
    n, d = x.shape
    return pl.pallas_call(
        _rmsnorm_kernel,
        grid=(n // tm,),
        in_specs=[pl.BlockSpec((tm, d), lambda i: (i, 0)),
                  pl.BlockSpec((1, d), lambda i: (0, 0))],
        out_specs=pl.BlockSpec((tm, d), lambda i: (i, 0)),
        out_shape=jax.ShapeDtypeStruct((n, d), F32),
        compiler_params=_cparams(("arbitrary",)),
        name="final_norm",
    )(x, g.reshape(1, -1))


def _routing_plan(top_idx, *, tm):
    n = top_idx.shape[0]
    n_assign = 2 * n
    rows = n_assign + N_EXPERTS * tm
    n_tiles = rows // tm
    e_flat = top_idx.reshape(n_assign)
    onehot = (e_flat[:, None] == jnp.arange(N_EXPERTS, dtype=jnp.int32)[None, :]).astype(jnp.int32)
    counts = jnp.sum(onehot, axis=0)
    rank = jnp.sum((jnp.cumsum(onehot, axis=0) - onehot) * onehot, axis=1)
    padded = ((counts + tm - 1) // tm) * tm
    ends = jnp.cumsum(padded)
    starts = ends - padded
    dest = starts[e_flat] + rank
    token = jnp.arange(n_assign, dtype=jnp.int32) // 2
    row_token = jnp.zeros((rows,), jnp.int32).at[dest].set(token)
    tile_start = jnp.arange(n_tiles, dtype=jnp.int32) * tm
    tile_expert = jnp.sum((tile_start[:, None] >= ends[None, :]).astype(jnp.int32), axis=1)
    tile_expert = jnp.minimum(tile_expert, N_EXPERTS - 1).astype(jnp.int32)
    n_used = (ends[-1] // tm).astype(jnp.int32).reshape(1)
    pos = dest.reshape(n, 2).astype(jnp.int32)
    experts = jnp.arange(N_EXPERTS, dtype=jnp.int32)
    later = (padded > 0)[None, :] & (experts[None, :] > experts[:, None])
    nxt = jnp.min(jnp.where(later, experts[None, :], N_EXPERTS), axis=1)
    next_expert = jnp.where(nxt == N_EXPERTS, -1, nxt).astype(jnp.int32)
    return row_token.reshape(n_tiles, tm), tile_expert, n_used, next_expert, pos


def _moe_layer(x, g, router_pad, wg, wu, wd, layer, final_g, *, final_norm):
    tm = 512
    idx, wt = _router(x, g, router_pad, tm=1024)
    row_token, tile_expert, n_used, next_expert, pos = _routing_plan(idx[:, :2], tm=tm)
    xs = _gather_norm(n_used, row_token, x, g, tr=tm)
    act = _moe_gate_up(tile_expert, n_used, next_expert, xs, wg, wu, layer, tm=tm, tn=1024)
    ys = _moe_down(tile_expert, n_used, act, wd, layer, tm=tm, tn=1024)
    return _combine(pos[:, 0], pos[:, 1], ys, x, wt, final_g, tr=512, final_norm=final_norm)


def _rope_tables(seq):
    pos = jnp.arange(seq, dtype=F32)[:, None]

    def tab(dim):
        inv = ROPE_THETA ** (-jnp.arange(0, dim, 2, dtype=F32) / dim)
        ang = pos * inv[None, :]
        return jnp.cos(ang), jnp.sin(ang)

    cf, sf = tab(HEAD_DIM)
    cos_f = jnp.concatenate([cf, cf], axis=1)
    sin_f = jnp.concatenate([-sf, sf], axis=1)
    cr, sr = tab(ROPE_DIM)
    zero = jnp.zeros((seq, LANES - ROPE_DIM), F32)
    cos_r = jnp.concatenate([cr, cr, zero], axis=1)
    sin_r = jnp.concatenate([-sr, sr, zero], axis=1)
    return cos_f, sin_f, cos_r, sin_r


def kernel(x, attn_norm_g, w_in, q_norm_g, kv_norm_g, w_uq, w_uk, w_uv, out_a_norm_g,
           out_b_norm_g, w_out, ffn_norm_g, dense_w_gate, dense_w_up, dense_w_down,
           router_w, moe_w_gate, moe_w_up, moe_w_down, final_norm_g):
    batch, seq, d_model = x.shape
    depth = w_in.shape[0]
    n = batch * seq
    mix = DIL_HEADS * HEAD_DIM
    cos_f, sin_f, cos_r, sin_r = _rope_tables(seq)
    dil_bias = _dil_bias(DIL_TQ)

    c0, c1, c2 = Q_LORA, Q_LORA + KV_LORA, Q_LORA + KV_LORA + ROPE_DIM
    lat_tiles = LATENT_COLS // PROJ_TN

    def proj_weight(wi):
        pad = jnp.zeros((d_model, 1024 - Q_LORA - ROPE_DIM), F32)
        return jnp.concatenate([wi[:, :c0], wi[:, c1:c2], pad, wi[:, c0:c1], wi[:, c2:]],
                               axis=1).astype(BF16)

    def q_up_weight(wq):
        uq = wq.reshape(Q_LORA, MLA_HEADS, QK_DIM)
        uq = jnp.concatenate([uq, jnp.zeros((Q_LORA, MLA_HEADS, 2 * LANES - QK_DIM), F32)], axis=2)
        return uq.reshape(Q_LORA, MLA_HEADS * 2 * LANES).astype(BF16)

    def bf16(w, idx):
        return w[idx].astype(BF16)

    moe_down_bf = moe_w_down.astype(BF16)

    xf = x.reshape(n, d_model)
    for layer in range(depth):
        g_attn = attn_norm_g[layer]
        proj = _norm_mm(xf, g_attn, proj_weight(w_in[layer]), cos_f, sin_f, x_block=(d_model, 0),
                        k_norm=d_model, tm=1024, tn=PROJ_TN,
                        rope_tiles=(lat_tiles, lat_tiles + 2 * mix // PROJ_TN),
                        scale_tiles=lat_tiles + mix // PROJ_TN,
                        scale=LOG2_E * HEAD_DIM ** -0.5, name="in_proj")
        q, kr = _q_up(proj, q_norm_g[layer], q_up_weight(w_uq[layer]), cos_r, sin_r, tm=1024,
                      tn=512, scale=LOG2_E * QK_DIM ** -0.5)
        w_ukv = jnp.concatenate([w_uk[layer], w_uv[layer]], axis=1).astype(BF16)
        kv = _norm_mm(proj, kv_norm_g[layer], w_ukv, cos_f, sin_f,
                      x_block=(KV_LORA, 1024 // KV_LORA), k_norm=KV_LORA, tm=1024, tn=1024,
                      name="kv_up")
        o_a = _mla_attention(q, kv, kr, batch=batch, seq=seq, tq=1024, tk=512)
        o_b = _dilated(proj, dil_bias, batch=batch, seq=seq, tq=DIL_TQ, ck=256)
        xf = _out_proj(o_a, o_b, out_a_norm_g[layer], out_b_norm_g[layer],
                       bf16(w_out, layer), xf, tm=1024, tn=512)
        j = layer // 2
        last = layer == depth - 1
        if layer % 2 == 0:
            act = _gate_up(xf, ffn_norm_g[layer], bf16(dense_w_gate, j), bf16(dense_w_up, j),
                           tm=1024, tn=512)
            xf = _down(act, bf16(dense_w_down, j), xf, tm=1024, tn=512)
            if last:
                xf = _final_norm(xf, final_norm_g, tm=1024)
        else:
            router_pad = jnp.concatenate(
                [router_w[j], jnp.zeros((d_model, LANES - N_EXPERTS), F32)], axis=1)
            xf = _moe_layer(xf, ffn_norm_g[layer], router_pad, moe_w_gate, moe_w_up, moe_down_bf,
                            j, final_norm_g, final_norm=last)
    return xf.reshape(batch, seq, d_model)
```

```python
import functools

import jax
import jax.numpy as jnp
from jax import lax
from jax.experimental import pallas as pl
from jax.experimental.pallas import tpu as pltpu

F32 = jnp.float32
BF16 = jnp.bfloat16

LANES = 128
BF16_SUBLANES = 16
MLA_HEADS = 8
DIL_HEADS = 8
HEAD_DIM = 128
Q_LORA = 768
KV_LORA = 512
ROPE_DIM = 64
QK_DIM = HEAD_DIM + ROPE_DIM
N_EXPERTS = 8
DIL_CONFIGS = ((128, 1), (512, 4), (2048, 16))
DIL_PAD = max(w for w, _ in DIL_CONFIGS)
DIL_TQ = 512
LATENT_COLS = 1536
PROJ_TN = 512
ROPE_THETA = 10000.0
EPS = 1e-6
NEG_INF = -1e30
LOG2_E = 1.4426950408889634
VMEM_LIMIT = 56 * 1024 * 1024


def _cparams(sem):
    return pltpu.CompilerParams(dimension_semantics=sem, vmem_limit_bytes=VMEM_LIMIT)


def _rms(xf, g):
    ms = jnp.mean(xf * xf, axis=-1, keepdims=True)
    return xf * lax.rsqrt(ms + EPS) * g


def _rope_full(a, c, s):
    return a * c + pltpu.roll(a, HEAD_DIM // 2, 1) * s


def _rope_half(a, c, s):
    lane = lax.broadcasted_iota(jnp.int32, a.shape, 1)
    lower = (lane % ROPE_DIM) < (ROPE_DIM // 2)
    partner = jnp.where(lower, pltpu.roll(a, LANES - ROPE_DIM // 2, 1),
                        pltpu.roll(a, ROPE_DIM // 2, 1))
    return a * c + partner * s


def _norm_mm_kernel(x_ref, g_ref, w_ref, cos_ref, sin_ref, o_ref, h_ref, *,
                    k_norm, rope_tiles, scale_tiles, scale):
    j = pl.program_id(1)

    @pl.when(j == 0)
    def _():
        xf = x_ref[:, :k_norm].astype(F32)
        h_ref[...] = _rms(xf, g_ref[...]).astype(BF16)

    acc = jnp.dot(h_ref[...], w_ref[...], preferred_element_type=F32)
    if rope_tiles is None:
        o_ref[...] = acc.astype(o_ref.dtype)
        return
    rope = (j >= rope_tiles[0]) & (j < rope_tiles[1])

    @pl.when(rope)
    def _():
        scaled = j < scale_tiles
        c = jnp.where(scaled, cos_ref[...] * scale, cos_ref[...])
        s = jnp.where(scaled, sin_ref[...] * scale, sin_ref[...])
        for k in range(acc.shape[1] // HEAD_DIM):
            sl = slice(k * HEAD_DIM, (k + 1) * HEAD_DIM)
            o_ref[:, sl] = _rope_full(acc[:, sl], c, s).astype(o_ref.dtype)

    @pl.when(jnp.logical_not(rope))
    def _():
        o_ref[...] = acc.astype(o_ref.dtype)


def _norm_mm(x, g, w, cos, sin, *, x_block, k_norm, tm, tn, rope_tiles=None, scale_tiles=0,
             scale=1.0, name):
    n = x.shape[0]
    xw, xb = x_block
    kw, nout = w.shape
    assert kw == k_norm and n % tm == 0 and nout % tn == 0
    ns = cos.shape[0] // tm
    kern = functools.partial(_norm_mm_kernel, k_norm=k_norm, rope_tiles=rope_tiles,
                             scale_tiles=scale_tiles, scale=scale)
    return pl.pallas_call(
        kern,
        grid=(n // tm, nout // tn),
        in_specs=[
            pl.BlockSpec((tm, xw), lambda i, j: (i, xb)),
            pl.BlockSpec((1, k_norm), lambda i, j: (0, 0)),
            pl.BlockSpec((k_norm, tn), lambda i, j: (0, j)),
            pl.BlockSpec((tm, HEAD_DIM), lambda i, j: (i % ns, 0)),
            pl.BlockSpec((tm, HEAD_DIM), lambda i, j: (i % ns, 0)),
        ],
        out_specs=pl.BlockSpec((tm, tn), lambda i, j: (i, j)),
        out_shape=jax.ShapeDtypeStruct((n, nout), BF16),
        scratch_shapes=[pltpu.VMEM((tm, k_norm), BF16)],
        compiler_params=_cparams(("arbitrary", "arbitrary")),
        name=name,
    )(x, g.reshape(1, -1), w, cos, sin)


def _q_up_kernel(x_ref, g_ref, w_ref, cos_ref, sin_ref, q_ref, kr_ref, h_ref, *, scale):
    j = pl.program_id(1)
    c = cos_ref[...]
    s = sin_ref[...]

    @pl.when(j == 0)
    def _():
        blk = x_ref[...].astype(F32)
        h_ref[...] = _rms(blk[:, :Q_LORA], g_ref[...]).astype(BF16)
        kr_ref[...] = _rope_half(blk[:, Q_LORA:Q_LORA + LANES], c, s).astype(BF16)

    acc = jnp.dot(h_ref[...], w_ref[...], preferred_element_type=F32)
    for hh in range(acc.shape[1] // (2 * LANES)):
        lo = hh * 2 * LANES
        q_ref[:, lo:lo + LANES] = (acc[:, lo:lo + LANES] * scale).astype(BF16)
        q_ref[:, lo + LANES:lo + 2 * LANES] = (
            _rope_half(acc[:, lo + LANES:lo + 2 * LANES], c, s) * scale).astype(BF16)


def _q_up(lat, g, w, cos, sin, *, tm, tn, scale):
    n = lat.shape[0]
    nout = w.shape[1]
    ns = cos.shape[0] // tm
    return pl.pallas_call(
        functools.partial(_q_up_kernel, scale=scale),
        grid=(n // tm, nout // tn),
        in_specs=[
            pl.BlockSpec((tm, 1024), lambda i, j: (i, 0)),
            pl.BlockSpec((1, Q_LORA), lambda i, j: (0, 0)),
            pl.BlockSpec((Q_LORA, tn), lambda i, j: (0, j)),
            pl.BlockSpec((tm, LANES), lambda i, j: (i % ns, 0)),
            pl.BlockSpec((tm, LANES), lambda i, j: (i % ns, 0)),
        ],
        out_specs=[
            pl.BlockSpec((tm, tn), lambda i, j: (i, j)),
            pl.BlockSpec((tm, LANES), lambda i, j: (i, 0)),
        ],
        out_shape=[jax.ShapeDtypeStruct((n, nout), BF16),
                   jax.ShapeDtypeStruct((n, LANES), BF16)],
        scratch_shapes=[pltpu.VMEM((tm, Q_LORA), BF16)],
        compiler_params=_cparams(("arbitrary", "arbitrary")),
        name="q_up",
    )(lat, g.reshape(1, -1), w, cos, sin)


def _mla_kernel(q_ref, kn_ref, kr_ref, v_ref, o_ref, kcat_ref, vt_ref, qt_ref, m_ref,
                acc_ref, s_ref, *, tq, tk):
    qi = pl.program_id(2)
    seq = kcat_ref.shape[0]
    vt_rows = vt_ref.shape[1]

    @pl.when(qi == 0)
    def _():
        kcat_ref[:, :HEAD_DIM] = kn_ref[...]
        kcat_ref[:, HEAD_DIM:] = kr_ref[...]
        ones_row = (lax.broadcasted_iota(jnp.int32, (vt_rows - HEAD_DIM, tk), 0) == 0)

        def transpose_chunk(c, carry):
            off = pl.multiple_of(c * tk, tk)
            vt_ref[c, :HEAD_DIM, :] = v_ref[pl.ds(off, tk), :].astype(F32).T.astype(BF16)
            vt_ref[c, HEAD_DIM:, :] = ones_row.astype(BF16)
            return carry

        lax.fori_loop(0, seq // tk, transpose_chunk, 0)

    qt_ref[...] = q_ref[...].astype(F32).T.astype(BF16)
    m_ref[...] = jnp.full(m_ref.shape, NEG_INF, F32)
    acc_ref[...] = jnp.zeros(acc_ref.shape, F32)
    per_q = tq // tk

    def scores(j, lo):
        off = pl.multiple_of(j * tk, tk)
        return jnp.dot(kcat_ref[pl.ds(off, tk), :], qt_ref[:, lo:],
                       preferred_element_type=F32)

    def chunk(j, masked, lo, next_lo):
        s = s_ref[j % 2, :, lo:]
        if next_lo is not None:
            s_ref[(j + 1) % 2, :, next_lo:] = scores(j + 1, next_lo)
        if masked:
            key = j * tk + lax.broadcasted_iota(jnp.int32, s.shape, 0)
            qry = qi * tq + lo + lax.broadcasted_iota(jnp.int32, s.shape, 1)
            s = jnp.where(key <= qry, s, NEG_INF)
        m_old = m_ref[:, lo:]
        m_new = jnp.maximum(m_old, jnp.max(s, axis=0, keepdims=True))
        a = jnp.exp2(m_old - m_new)
        p = jnp.exp2(s - m_new).astype(BF16)
        acc_ref[:, lo:] = a * acc_ref[:, lo:] + jnp.dot(vt_ref[j], p,
                                                        preferred_element_type=F32)
        m_ref[:, lo:] = m_new

    def body(j, carry):
        chunk(j, False, 0, 0)
        return carry

    n_full = qi * per_q
    s_ref[0] = scores(0, 0)
    lax.fori_loop(0, n_full, body, 0)
    for d in range(per_q):
        chunk(n_full + d, True, d * tk, (d + 1) * tk if d + 1 < per_q else None)
    out_t = acc_ref[:HEAD_DIM, :] / acc_ref[HEAD_DIM:HEAD_DIM + 1, :]
    o_ref[...] = out_t.T.astype(o_ref.dtype)


def _mla_attention(q, kv, kr, *, batch, seq, tq, tk):
    n = q.shape[0]
    nq = seq // tq
    return pl.pallas_call(
        functools.partial(_mla_kernel, tq=tq, tk=tk),
        grid=(batch, MLA_HEADS, nq),
        in_specs=[
            pl.BlockSpec((tq, 2 * LANES), lambda b, h, i: (b * nq + i, h)),
            pl.BlockSpec((seq, HEAD_DIM), lambda b, h, i: (b, h)),
            pl.BlockSpec((seq, LANES), lambda b, h, i: (b, 0)),
            pl.BlockSpec((seq, HEAD_DIM), lambda b, h, i: (b, MLA_HEADS + h)),
        ],
        out_specs=pl.BlockSpec((tq, HEAD_DIM), lambda b, h, i: (b * nq + i, h)),
        out_shape=jax.ShapeDtypeStruct((n, MLA_HEADS * HEAD_DIM), BF16),
        scratch_shapes=[
            pltpu.VMEM((seq, 2 * LANES), BF16),
            pltpu.VMEM((seq // tk, HEAD_DIM + BF16_SUBLANES, tk), BF16),
            pltpu.VMEM((2 * LANES, tq), BF16),
            pltpu.VMEM((1, tq), F32),
            pltpu.VMEM((HEAD_DIM + BF16_SUBLANES, tq), F32),
            pltpu.VMEM((2, tk, tq), F32),
        ],
        compiler_params=_cparams(("arbitrary", "arbitrary", "arbitrary")),
        name="mla_attention",
    )(q, kv, kr, kv)


def _dil_bias(tq):
    kk = jnp.arange(DIL_PAD + tq, dtype=jnp.int32)[:, None]
    qq = jnp.arange(tq, dtype=jnp.int32)[None, :]
    delta = qq + DIL_PAD - kk
    count = jnp.zeros(delta.shape, F32)
    for win, dil in DIL_CONFIGS:
        count += ((delta >= 0) & (delta <= win) & (delta % dil == 0)).astype(F32)
    return jnp.where(count > 0, jnp.log2(jnp.maximum(count, 1.0)), NEG_INF)


def _dil_kernel(q_ref, k_ref, v_ref, bias_ref, o_ref, kpad_ref, vt_ref, s_ref, *, tq):
    i = pl.program_id(2)
    seq = k_ref.shape[0]
    ck = vt_ref.shape[2]
    vt_rows = vt_ref.shape[1]
    n_pad = DIL_PAD // ck
    win = DIL_PAD + tq

    @pl.when(i == 0)
    def _():
        kpad_ref[:DIL_PAD, :] = jnp.zeros((DIL_PAD, HEAD_DIM), BF16)
        kpad_ref[DIL_PAD:, :] = k_ref[...]
        ones_row = (lax.broadcasted_iota(jnp.int32, (vt_rows - HEAD_DIM, ck), 0) == 0)
        for c in range(n_pad):
            vt_ref[c] = jnp.zeros((vt_rows, ck), BF16)

        def transpose_chunk(c, carry):
            off = pl.multiple_of(c * ck, ck)
            vt_ref[n_pad + c, :HEAD_DIM, :] = v_ref[pl.ds(off, ck), :].astype(F32).T.astype(BF16)
            vt_ref[n_pad + c, HEAD_DIM:, :] = ones_row.astype(BF16)
            return carry

        lax.fori_loop(0, seq // ck, transpose_chunk, 0)

    s0 = pl.multiple_of(i * tq, tq)
    qt = q_ref[...].astype(F32).T.astype(BF16)

    def attend(skip):
        rows = win - skip
        s_ref[:rows, :] = jnp.dot(kpad_ref[pl.ds(s0 + skip, rows), :], qt,
                                  preferred_element_type=F32) + bias_ref[skip:, :]
        s = s_ref[:rows, :]
        m = jnp.max(s, axis=0, keepdims=True)
        p = jnp.exp2(s - m).astype(BF16)
        base = i * (tq // ck) + skip // ck
        acc = jnp.zeros((vt_rows, tq), F32)
        for c in range(rows // ck):
            acc += jnp.dot(vt_ref[base + c], p[c * ck:(c + 1) * ck, :],
                           preferred_element_type=F32)
        o_ref[...] = (acc[:HEAD_DIM, :] / acc[HEAD_DIM:HEAD_DIM + 1, :]).T

    n_short = DIL_PAD // tq
    for t in range(n_short):
        pl.when(i == t)(functools.partial(attend, DIL_PAD - t * tq))
    pl.when(i >= n_short)(functools.partial(attend, 0))


def _dilated(qkv, bias, *, batch, seq, tq, ck):
    n = qkv.shape[0]
    nq = seq // tq
    win = DIL_PAD + tq
    vt_rows = HEAD_DIM + BF16_SUBLANES
    q0 = LATENT_COLS // HEAD_DIM
    return pl.pallas_call(
        functools.partial(_dil_kernel, tq=tq),
        grid=(batch, DIL_HEADS, nq),
        in_specs=[
            pl.BlockSpec((tq, HEAD_DIM), lambda b, h, i: (b * nq + i, q0 + h)),
            pl.BlockSpec((seq, HEAD_DIM), lambda b, h, i: (b, q0 + DIL_HEADS + h)),
            pl.BlockSpec((seq, HEAD_DIM), lambda b, h, i: (b, q0 + 2 * DIL_HEADS + h)),
            pl.BlockSpec((win, tq), lambda b, h, i: (0, 0)),
        ],
        out_specs=pl.BlockSpec((tq, HEAD_DIM), lambda b, h, i: (b * nq + i, h)),
        out_shape=jax.ShapeDtypeStruct((n, DIL_HEADS * HEAD_DIM), F32),
        scratch_shapes=[
            pltpu.VMEM((DIL_PAD + seq, HEAD_DIM), BF16),
            pltpu.VMEM(((DIL_PAD + seq) // ck, vt_rows, ck), BF16),
            pltpu.VMEM((win, tq), F32),
        ],
        compiler_params=_cparams(("arbitrary", "arbitrary", "arbitrary")),
        name="dilated_attention",
    )(qkv, qkv, qkv, bias)


def _out_proj_kernel(oa_ref, ob_ref, ga_ref, gb_ref, w_ref, x_ref, o_ref, h_ref):
    half = oa_ref.shape[1]

    @pl.when(pl.program_id(1) == 0)
    def _():
        h_ref[:, :half] = _rms(oa_ref[...].astype(F32), ga_ref[...]).astype(BF16)
        h_ref[:, half:] = _rms(ob_ref[...], gb_ref[...]).astype(BF16)

    o_ref[...] = x_ref[...] + jnp.dot(h_ref[...], w_ref[...], preferred_element_type=F32)


def _out_proj(o_a, o_b, g_a, g_b, w, x, *, tm, tn):
    n, d = x.shape
    half = o_a.shape[1]
    row = lambda i, j: (i, 0)
    return pl.pallas_call(
        _out_proj_kernel,
        grid=(n // tm, d // tn),
        in_specs=[pl.BlockSpec((tm, half), row)] * 2
        + [pl.BlockSpec((1, half), lambda i, j: (0, 0))] * 2
        + [pl.BlockSpec((2 * half, tn), lambda i, j: (0, j)),
           pl.BlockSpec((tm, tn), lambda i, j: (i, j))],
        out_specs=pl.BlockSpec((tm, tn), lambda i, j: (i, j)),
        out_shape=jax.ShapeDtypeStruct((n, d), F32),
        scratch_shapes=[pltpu.VMEM((tm, 2 * half), BF16)],
        compiler_params=_cparams(("arbitrary", "arbitrary")),
        name="out_proj",
    )(o_a, o_b, g_a.reshape(1, -1), g_b.reshape(1, -1), w, x)


def _gate_up_kernel(x_ref, g_ref, wg_ref, wu_ref, o_ref, h_ref):
    @pl.when(pl.program_id(1) == 0)
    def _():
        h_ref[...] = _rms(x_ref[...], g_ref[...]).astype(BF16)

    h = h_ref[...]
    gate = jnp.dot(h, wg_ref[...], preferred_element_type=F32)
    up = jnp.dot(h, wu_ref[...], preferred_element_type=F32)
    o_ref[...] = (gate * jax.nn.sigmoid(gate) * up).astype(o_ref.dtype)


def _gate_up(x, g, wg, wu, *, tm, tn):
    n, d = x.shape
    f = wg.shape[1]
    return pl.pallas_call(
        _gate_up_kernel,
        grid=(n // tm, f // tn),
        in_specs=[
            pl.BlockSpec((tm, d), lambda i, j: (i, 0)),
            pl.BlockSpec((1, d), lambda i, j: (0, 0)),
            pl.BlockSpec((d, tn), lambda i, j: (0, j)),
            pl.BlockSpec((d, tn), lambda i, j: (0, j)),
        ],
        out_specs=pl.BlockSpec((tm, tn), lambda i, j: (i, j)),
        out_shape=jax.ShapeDtypeStruct((n, f), BF16),
        scratch_shapes=[pltpu.VMEM((tm, d), BF16)],
        compiler_params=_cparams(("arbitrary", "arbitrary")),
        name="ffn_gate_up",
    )(x, g.reshape(1, -1), wg, wu)


def _down_kernel(a_ref, w_ref, x_ref, o_ref):
    o_ref[...] = x_ref[...] + jnp.dot(a_ref[...], w_ref[...], preferred_element_type=F32)


def _down(act, w, x, *, tm, tn):
    n, f = act.shape
    d = w.shape[1]
    return pl.pallas_call(
        _down_kernel,
        grid=(n // tm, d // tn),
        in_specs=[
            pl.BlockSpec((tm, f), lambda i, j: (i, 0)),
            pl.BlockSpec((f, tn), lambda i, j: (0, j)),
            pl.BlockSpec((tm, tn), lambda i, j: (i, j)),
        ],
        out_specs=pl.BlockSpec((tm, tn), lambda i, j: (i, j)),
        out_shape=jax.ShapeDtypeStruct((n, d), F32),
        compiler_params=_cparams(("arbitrary", "arbitrary")),
        name="ffn_down",
    )(act, w, x)


def _router_kernel(x_ref, g_ref, w_ref, idx_ref, wt_ref):
    h = _rms(x_ref[...], g_ref[...])
    logits = jnp.dot(h, w_ref[...], preferred_element_type=F32,
                     precision=lax.Precision.HIGHEST)
    lane_i = lax.broadcasted_iota(jnp.int32, logits.shape, 1)
    lane = lane_i.astype(F32)
    logits = jnp.where(lane_i < N_EXPERTS, logits, -jnp.inf)
    v1 = jnp.max(logits, axis=-1, keepdims=True)
    i1 = jnp.min(jnp.where(logits == v1, lane, float(LANES)), axis=-1, keepdims=True)
    rest = jnp.where(lane == i1, -jnp.inf, logits)
    v2 = jnp.max(rest, axis=-1, keepdims=True)
    i2 = jnp.min(jnp.where(rest == v2, lane, float(LANES)), axis=-1, keepdims=True)
    e2 = jnp.exp(v2 - v1)
    w1 = 1.0 / (1.0 + e2)
    w2 = e2 / (1.0 + e2)
    idx_ref[...] = jnp.where(lane_i == 0, i1, jnp.where(lane_i == 1, i2, 0.0)).astype(jnp.int32)
    wt_ref[...] = jnp.where(lane_i == 0, w1, jnp.where(lane_i == 1, w2, 0.0))


def _router(x, g, w_pad, *, tm):
    n, d = x.shape
    return pl.pallas_call(
        _router_kernel,
        grid=(n // tm,),
        in_specs=[
            pl.BlockSpec((tm, d), lambda i: (i, 0)),
            pl.BlockSpec((1, d), lambda i: (0, 0)),
            pl.BlockSpec((d, LANES), lambda i: (0, 0)),
        ],
        out_specs=[pl.BlockSpec((tm, LANES), lambda i: (i, 0))] * 2,
        out_shape=[jax.ShapeDtypeStruct((n, LANES), jnp.int32),
                   jax.ShapeDtypeStruct((n, LANES), F32)],
        compiler_params=_cparams(("arbitrary",)),
        name="moe_router",
    )(x, g.reshape(1, -1), w_pad)


def _row_copy(src_hbm, row, dst_vmem, slot, sem):
    return pltpu.make_async_copy(src_hbm.at[pl.ds(row, 1)], dst_vmem.at[pl.ds(slot, 1)], sem)


DMA_UNROLL = 8
DMA_PRIORITIES = 2


def _load_indices(idx_hbm, tile, idx_smem, isem):
    cp = pltpu.make_async_copy(idx_hbm.at[tile], idx_smem, isem)
    cp.start()
    cp.wait()


def _start_rows(idx_smem, src_hbm, dst_vmem, sem):
    def start(i, c):
        for prio in range(DMA_PRIORITIES):
            row = DMA_PRIORITIES * i + prio
            _row_copy(src_hbm, idx_smem[row], dst_vmem, row, sem).start(priority=prio)
        return c

    lax.fori_loop(0, dst_vmem.shape[0] // DMA_PRIORITIES, start, 0,
                  unroll=DMA_UNROLL // DMA_PRIORITIES)


def _wait_rows(src_hbm, dst_vmem, sem):
    def wait(i, c):
        _row_copy(src_hbm, 0, dst_vmem, i, sem).wait()
        return c

    lax.fori_loop(0, dst_vmem.shape[0], wait, 0, unroll=DMA_UNROLL)


def _gather_norm_kernel(nu_ref, idx_hbm, x_hbm, g_ref, o_ref, idx_smem, buf, isem, sem):
    i = pl.program_id(0)
    n_used = nu_ref[0]

    def fetch(tile, slot):
        _load_indices(idx_hbm, tile, idx_smem, isem)
        _start_rows(idx_smem, x_hbm, buf.at[slot], sem.at[slot])

    @pl.when((i == 0) & (n_used > 0))
    def _():
        fetch(0, 0)

    @pl.when(i + 1 < n_used)
    def _():
        fetch(i + 1, (i + 1) % 2)

    @pl.when(i < n_used)
    def _():
        slot = i % 2
        _wait_rows(x_hbm, buf.at[slot], sem.at[slot])
        o_ref[...] = _rms(buf[slot], g_ref[...]).astype(o_ref.dtype)

    @pl.when(i >= n_used)
    def _():
        o_ref[...] = jnp.zeros(o_ref.shape, o_ref.dtype)


def _gather_norm(n_used, row_token, x, g, *, tr):
    nt = row_token.shape[0]
    d = x.shape[1]
    grid_spec = pltpu.PrefetchScalarGridSpec(
        num_scalar_prefetch=1,
        grid=(nt,),
        in_specs=[
            pl.BlockSpec(memory_space=pl.ANY),
            pl.BlockSpec(memory_space=pl.ANY),
            pl.BlockSpec((1, d), lambda i, nu: (0, 0)),
        ],
        out_specs=pl.BlockSpec((tr, d), lambda i, nu: (i, 0)),
        scratch_shapes=[
            pltpu.SMEM((tr,), jnp.int32),
            pltpu.VMEM((2, tr, d), F32),
            pltpu.SemaphoreType.DMA,
            pltpu.SemaphoreType.DMA((2,)),
        ],
    )
    return pl.pallas_call(
        _gather_norm_kernel,
        grid_spec=grid_spec,
        out_shape=jax.ShapeDtypeStruct((nt * tr, d), BF16),
        compiler_params=_cparams(("arbitrary",)),
        name="moe_gather",
    )(n_used, row_token, x, g.reshape(1, -1))


def _moe_gate_up_kernel(te_ref, nu_ref, nxt_ref, x_ref, wg_hbm, wu_hbm, o_ref, stage_g, stage_u,
                        wg_bf, wu_bf, sem, *, layer, tn):
    j = pl.program_id(0)
    i = pl.program_id(1)
    n_col = pl.num_programs(0)
    used = i < nu_ref[0]
    e = te_ref[i]
    fresh = (i == 0) | (e != te_ref[jnp.maximum(i - 1, 0)])

    def window_copies(col, expert):
        cols = pl.ds(pl.multiple_of(col * tn, tn), tn)
        return (pltpu.make_async_copy(wg_hbm.at[layer, expert, :, cols], stage_g, sem.at[0]),
                pltpu.make_async_copy(wu_hbm.at[layer, expert, :, cols], stage_u, sem.at[1]))

    def start_window(col, expert):
        for cp in window_copies(col, expert):
            cp.start()

    @pl.when((j == 0) & (i == 0) & used)
    def _():
        start_window(0, e)

    @pl.when(used & fresh)
    def _():
        for cp in window_copies(j, e):
            cp.wait()
        wg_bf[...] = stage_g[...].astype(BF16)
        wu_bf[...] = stage_u[...].astype(BF16)
        nxt = nxt_ref[e]

        @pl.when(nxt >= 0)
        def _():
            start_window(j, nxt)

        @pl.when((nxt < 0) & (j + 1 < n_col))
        def _():
            start_window(j + 1, te_ref[0])

    @pl.when(used)
    def _():
        h = x_ref[...]
        gate = jnp.dot(h, wg_bf[...], preferred_element_type=F32)
        up = jnp.dot(h, wu_bf[...], preferred_element_type=F32)
        o_ref[...] = (gate * jax.nn.sigmoid(gate) * up).astype(o_ref.dtype)

    @pl.when(jnp.logical_not(used))
    def _():
        o_ref[...] = jnp.zeros(o_ref.shape, o_ref.dtype)


def _moe_gate_up(tile_expert, n_used, next_expert, xs, wg, wu, layer, *, tm, tn):
    r, d = xs.shape
    f = wg.shape[3]
    grid_spec = pltpu.PrefetchScalarGridSpec(
        num_scalar_prefetch=3,
        grid=(f // tn, r // tm),
        in_specs=[pl.BlockSpec((tm, d), lambda j, i, te, nu, nx: (i, 0)),
                  pl.BlockSpec(memory_space=pl.ANY),
                  pl.BlockSpec(memory_space=pl.ANY)],
        out_specs=pl.BlockSpec((tm, tn), lambda j, i, te, nu, nx: (i, j)),
        scratch_shapes=[pltpu.VMEM((d, tn), F32), pltpu.VMEM((d, tn), F32),
                        pltpu.VMEM((d, tn), BF16), pltpu.VMEM((d, tn), BF16),
                        pltpu.SemaphoreType.DMA((2,))],
    )
    return pl.pallas_call(
        functools.partial(_moe_gate_up_kernel, layer=layer, tn=tn),
        grid_spec=grid_spec,
        out_shape=jax.ShapeDtypeStruct((r, f), BF16),
        compiler_params=_cparams(("arbitrary", "arbitrary")),
        name="moe_gate_up",
    )(tile_expert, n_used, next_expert, xs, wg, wu)


def _moe_down_kernel(te_ref, nu_ref, a_ref, w_ref, o_ref):
    i = pl.program_id(1)

    @pl.when(i < nu_ref[0])
    def _():
        o_ref[...] = jnp.dot(a_ref[...], w_ref[0], preferred_element_type=F32)

    @pl.when(i >= nu_ref[0])
    def _():
        o_ref[...] = jnp.zeros(o_ref.shape, o_ref.dtype)


def _moe_down(tile_expert, n_used, act, w, layer, *, tm, tn):
    r, f = act.shape
    d = w.shape[3]
    grid_spec = pltpu.PrefetchScalarGridSpec(
        num_scalar_prefetch=2,
        grid=(d // tn, r // tm),
        in_specs=[
            pl.BlockSpec((tm, f), lambda j, i, te, nu: (i, 0)),
            pl.BlockSpec((None, 1, f, tn), lambda j, i, te, nu: (layer, te[i], 0, j)),
        ],
        out_specs=pl.BlockSpec((tm, tn), lambda j, i, te, nu: (i, j)),
    )
    return pl.pallas_call(
        _moe_down_kernel,
        grid_spec=grid_spec,
        out_shape=jax.ShapeDtypeStruct((r, d), F32),
        compiler_params=_cparams(("arbitrary", "arbitrary")),
        name="moe_down",
    )(tile_expert, n_used, act, w)


def _combine_kernel(p0_hbm, p1_hbm, ys_hbm, x_ref, wt_ref, g_ref, o_ref, i0_smem, i1_smem,
                    b0, b1, isem, sem0, sem1, *, final_norm):
    i = pl.program_id(0)
    n = pl.num_programs(0)

    def fetch(tile, slot):
        loads = (pltpu.make_async_copy(p0_hbm.at[tile], i0_smem, isem.at[0]),
                 pltpu.make_async_copy(p1_hbm.at[tile], i1_smem, isem.at[1]))
        for cp in loads:
            cp.start()
        for cp in loads:
            cp.wait()
        _start_rows(i0_smem, ys_hbm, b0.at[slot], sem0.at[slot])
        _start_rows(i1_smem, ys_hbm, b1.at[slot], sem1.at[slot])

    @pl.when(i == 0)
    def _():
        fetch(0, 0)

    @pl.when(i + 1 < n)
    def _():
        fetch(i + 1, (i + 1) % 2)

    slot = i % 2
    _wait_rows(ys_hbm, b0.at[slot], sem0.at[slot])
    _wait_rows(ys_hbm, b1.at[slot], sem1.at[slot])
    wt = wt_ref[...]
    y = x_ref[...] + (wt[:, 0:1] * b0[slot] + wt[:, 1:2] * b1[slot])
    if final_norm:
        y = _rms(y, g_ref[...])
    o_ref[...] = y


def _combine(pos0, pos1, ys, x, wt, g, *, tr, final_norm):
    n, d = x.shape
    return pl.pallas_call(
        functools.partial(_combine_kernel, final_norm=final_norm),
        grid=(n // tr,),
        in_specs=[
            pl.BlockSpec(memory_space=pl.ANY),
            pl.BlockSpec(memory_space=pl.ANY),
            pl.BlockSpec(memory_space=pl.ANY),
            pl.BlockSpec((tr, d), lambda i: (i, 0)),
            pl.BlockSpec((tr, LANES), lambda i: (i, 0)),
            pl.BlockSpec((1, d), lambda i: (0, 0)),
        ],
        out_specs=pl.BlockSpec((tr, d), lambda i: (i, 0)),
        out_shape=jax.ShapeDtypeStruct((n, d), F32),
        scratch_shapes=[
            pltpu.SMEM((tr,), jnp.int32),
            pltpu.SMEM((tr,), jnp.int32),
            pltpu.VMEM((2, tr, d), F32),
            pltpu.VMEM((2, tr, d), F32),
            pltpu.SemaphoreType.DMA((2,)),
            pltpu.SemaphoreType.DMA((2,)),
            pltpu.SemaphoreType.DMA((2,)),
        ],
        compiler_params=_cparams(("arbitrary",)),
        name="moe_combine",
    )(pos0.reshape(n // tr, tr), pos1.reshape(n // tr, tr), ys, x, wt, g.reshape(1, -1))


def _rmsnorm_kernel(x_ref, g_ref, o_ref):
    o_ref[...] = _rms(x_ref[...], g_ref[...])


def _final_norm(x, g, *, tm):
    n, d = x.shape
    return pl.pallas_call(
        _rmsnorm_kernel,
        grid=(n // tm,),
        in_specs=[pl.BlockSpec((tm, d), lambda i: (i, 0)),
                  pl.BlockSpec((1, d), lambda i: (0, 0))],
        out_specs=pl.BlockSpec((tm, d), lambda i: (i, 0)),
        out_shape=jax.ShapeDtypeStruct((n, d), F32),
        compiler_params=_cparams(("arbitrary",)),
        name="final_norm",
    )(x, g.reshape(1, -1))


def _routing_plan(top_idx, *, tm):
    n = top_idx.shape[0]
    n_assign = 2 * n
    rows = n_assign + N_EXPERTS * tm
    n_tiles = rows // tm
    e_flat = top_idx.reshape(n_assign)
    onehot = (e_flat[:, None] == jnp.arange(N_EXPERTS, dtype=jnp.int32)[None, :]).astype(jnp.int32)
    counts = jnp.sum(onehot, axis=0)
    rank = jnp.sum((jnp.cumsum(onehot, axis=0) - onehot) * onehot, axis=1)
    padded = ((counts + tm - 1) // tm) * tm
    ends = jnp.cumsum(padded)
    starts = ends - padded
    dest = starts[e_flat] + rank
    token = jnp.arange(n_assign, dtype=jnp.int32) // 2
    row_token = jnp.zeros((rows,), jnp.int32).at[dest].set(token)
    tile_start = jnp.arange(n_tiles, dtype=jnp.int32) * tm
    tile_expert = jnp.sum((tile_start[:, None] >= ends[None, :]).astype(jnp.int32), axis=1)
    tile_expert = jnp.minimum(tile_expert, N_EXPERTS - 1).astype(jnp.int32)
    n_used = (ends[-1] // tm).astype(jnp.int32).reshape(1)
    pos = dest.reshape(n, 2).astype(jnp.int32)
    experts = jnp.arange(N_EXPERTS, dtype=jnp.int32)
    later = (padded > 0)[None, :] & (experts[None, :] > experts[:, None])
    nxt = jnp.min(jnp.where(later, experts[None, :], N_EXPERTS), axis=1)
    next_expert = jnp.where(nxt == N_EXPERTS, -1, nxt).astype(jnp.int32)
    return row_token.reshape(n_tiles, tm), tile_expert, n_used, next_expert, pos


def _moe_layer(x, g, router_pad, wg, wu, wd, layer, final_g, *, final_norm):
    tm = 512
    idx, wt = _router(x, g, router_pad, tm=1024)
    row_token, tile_expert, n_used, next_expert, pos = _routing_plan(idx[:, :2], tm=tm)
    xs = _gather_norm(n_used, row_token, x, g, tr=tm)
    act = _moe_gate_up(tile_expert, n_used, next_expert, xs, wg, wu, layer, tm=tm, tn=1024)
    ys = _moe_down(tile_expert, n_used, act, wd, layer, tm=tm, tn=1024)
    return _combine(pos[:, 0], pos[:, 1], ys, x, wt, final_g, tr=512, final_norm=final_norm)


def _rope_tables(seq):
    pos = jnp.arange(seq, dtype=F32)[:, None]

    def tab(dim):
        inv = ROPE_THETA ** (-jnp.arange(0, dim, 2, dtype=F32) / dim)
        ang = pos * inv[None, :]
        return jnp.cos(ang), jnp.sin(ang)

    cf, sf = tab(HEAD_DIM)
    cos_f = jnp.concatenate([cf, cf], axis=1)
    sin_f = jnp.concatenate([-sf, sf], axis=1)
    cr, sr = tab(ROPE_DIM)
    zero = jnp.zeros((seq, LANES - ROPE_DIM), F32)
    cos_r = jnp.concatenate([cr, cr, zero], axis=1)
    sin_r = jnp.concatenate([-sr, sr, zero], axis=1)
    return cos_f, sin_f, cos_r, sin_r


def kernel(x, attn_norm_g, w_in, q_norm_g, kv_norm_g, w_uq, w_uk, w_uv, out_a_norm_g,
           out_b_norm_g, w_out, ffn_norm_g, dense_w_gate, dense_w_up, dense_w_down,
           router_w, moe_w_gate, moe_w_up, moe_w_down, final_norm_g):
    batch, seq, d_model = x.shape
    depth = w_in.shape[0]
    n = batch * seq
    mix = DIL_HEADS * HEAD_DIM
    cos_f, sin_f, cos_r, sin_r = _rope_tables(seq)
    dil_bias = _dil_bias(DIL_TQ)

    c0, c1, c2 = Q_LORA, Q_LORA + KV_LORA, Q_LORA + KV_LORA + ROPE_DIM
    lat_tiles = LATENT_COLS // PROJ_TN

    def proj_weight(wi):
        pad = jnp.zeros((d_model, 1024 - Q_LORA - ROPE_DIM), F32)
        return jnp.concatenate([wi[:, :c0], wi[:, c1:c2], pad, wi[:, c0:c1], wi[:, c2:]],
                               axis=1).astype(BF16)

    def q_up_weight(wq):
        uq = wq.reshape(Q_LORA, MLA_HEADS, QK_DIM)
        uq = jnp.concatenate([uq, jnp.zeros((Q_LORA, MLA_HEADS, 2 * LANES - QK_DIM), F32)], axis=2)
        return uq.reshape(Q_LORA, MLA_HEADS * 2 * LANES).astype(BF16)

    def bf16(w, idx):
        return w[idx].astype(BF16)

    moe_down_bf = moe_w_down.astype(BF16)

    xf = x.reshape(n, d_model)
    for layer in range(depth):
        g_attn = attn_norm_g[layer]
        proj = _norm_mm(xf, g_attn, proj_weight(w_in[layer]), cos_f, sin_f, x_block=(d_model, 0),
                        k_norm=d_model, tm=1024, tn=PROJ_TN,
                        rope_tiles=(lat_tiles, lat_tiles + 2 * mix // PROJ_TN),
                        scale_tiles=lat_tiles + mix // PROJ_TN,
                        scale=LOG2_E * HEAD_DIM ** -0.5, name="in_proj")
        q, kr = _q_up(proj, q_norm_g[layer], q_up_weight(w_uq[layer]), cos_r, sin_r, tm=1024,
                      tn=1024, scale=LOG2_E * QK_DIM ** -0.5)
        w_ukv = jnp.concatenate([w_uk[layer], w_uv[layer]], axis=1).astype(BF16)
        kv = _norm_mm(proj, kv_norm_g[layer], w_ukv, cos_f, sin_f,
                      x_block=(KV_LORA, 1024 // KV_LORA), k_norm=KV_LORA, tm=1024, tn=2048,
                      name="kv_up")
        o_a = _mla_attention(q, kv, kr, batch=batch, seq=seq, tq=1024, tk=512)
        o_b = _dilated(proj, dil_bias, batch=batch, seq=seq, tq=DIL_TQ, ck=256)
        xf = _out_proj(o_a, o_b, out_a_norm_g[layer], out_b_norm_g[layer],
                       bf16(w_out, layer), xf, tm=1024, tn=1024)
        j = layer // 2
        last = layer == depth - 1
        if layer % 2 == 0:
            act = _gate_up(xf, ffn_norm_g[layer], bf16(dense_w_gate, j), bf16(dense_w_up, j),
                           tm=1024, tn=512)
            xf = _down(act, bf16(dense_w_down, j), xf, tm=1024, tn=512)
            if last:
                xf = _final_norm(xf, final_norm_g, tm=1024)
        else:
            router_pad = jnp.concatenate(
                [router_w[j], jnp.zeros((d_model, LANES - N_EXPERTS), F32)], axis=1)
            xf = _moe_layer(xf, ffn_norm_g[layer], router_pad, moe_w_gate, moe_w_up, moe_down_bf,
                            j, final_norm_g, final_norm=last)
    return xf.reshape(batch, seq, d_model)
```
